```python
import jax, jax.numpy as jnp
from jax import lax
import numpy as np

D_MODEL = 2048
BATCH = 8
SEQ = 2048
DEPTH = 1
DEC_BATCH = 32
DEC_SEQ = 4
PAST_LEN = 16384
PAGE_SIZE = 128

N_META = 16
N_HEADS = 16
HEAD_DIM = 128
D_ATT = N_HEADS * HEAD_DIM
D_CONV = D_MODEL // 2
CONV_WIDTH = 31
D_FF = 256 * ((8 * D_MODEL // 3 + 255) // 256)
BLOCK = 128
PAD_FRONT = (-N_META) % BLOCK
ALPHA = (2.0 * DEPTH) ** 0.25
BETA = (8.0 * DEPTH) ** -0.25
FORGET_BIAS = 6.0
FORGET_W_SCALE = 0.5
LN_EPS = 1e-5
NEG_INF = -1e30
SPLITS = [2 * D_CONV, D_ATT, D_ATT, D_ATT, N_HEADS, D_MODEL, D_MODEL]
OFFS = [int(o) for o in np.cumsum(SPLITS)]
D_IN = OFFS[-1]

kernel_name = "hybrid_conformer_fox_decoder_step"


def layer_norm(x, g, b):
    xf = x.astype(jnp.float32)
    mu = jnp.mean(xf, axis=-1, keepdims=True)
    var = jnp.mean(jnp.square(xf - mu), axis=-1, keepdims=True)
    return ((xf - mu) * lax.rsqrt(var + LN_EPS) * g.astype(jnp.float32) + b.astype(jnp.float32)).astype(x.dtype)


def swiglu(x, w_in, w_out):
    a, b = jnp.split(x @ w_in, 2, axis=-1)
    return (jax.nn.silu(a) * b) @ w_out


def ffn_sublayer(x, w_in, w_out, g, b):
    return layer_norm(ALPHA * x + 0.5 * swiglu(x, w_in, w_out), g, b)


def branch_inputs(x, w_in, b_in):
    z = x @ w_in + b_in
    u2, q, k, v, f, gc, ga = jnp.split(z, OFFS[:-1], axis=-1)
    u = u2[..., :D_CONV] * jax.nn.sigmoid(u2[..., D_CONV:])
    shp = x.shape[:-1] + (N_HEADS, HEAD_DIM)
    logf = jax.nn.log_sigmoid(f.astype(jnp.float32))
    return u, q.reshape(shp), k.reshape(shp), v.reshape(shp), logf, jax.nn.sigmoid(gc), jax.nn.sigmoid(ga)


def conv_branch(u_hist, conv_w, conv_b, ln_g, ln_b, w_out):
    h = lax.conv_general_dilated(u_hist, conv_w[:, None, :].astype(u_hist.dtype), (1,), 'VALID',
                                 dimension_numbers=('NWC', 'WIO', 'NWC'), feature_group_count=D_CONV)
    h = jax.nn.silu(layer_norm(h + conv_b, ln_g, ln_b))
    return h @ w_out


def merge_branches(conv_d, att, g_conv, g_att, w_att_proj, w_o):
    att_d = att.reshape(att.shape[:-2] + (D_ATT,)) @ w_att_proj
    return (g_conv * conv_d + g_att * att_d) @ w_o


def fox_attention_prompt(q, k, v, logf):
    B, L = q.shape[:2]
    Lp = L + PAD_FRONT
    pw = ((0, 0), (PAD_FRONT, 0), (0, 0), (0, 0))
    qp, kp, vp = jnp.pad(q, pw), jnp.pad(k, pw), jnp.pad(v, pw)
    dk = jnp.cumsum(jnp.pad(logf, ((0, 0), (PAD_FRONT, 0), (0, 0))), axis=1).transpose(0, 2, 1)
    key_idx = jnp.arange(Lp)
    scale = HEAD_DIM ** -0.5

    def one_block(i):
        q0 = i * BLOCK
        qb = lax.dynamic_slice_in_dim(qp, q0, BLOCK, axis=1)
        db = lax.dynamic_slice_in_dim(dk, q0, BLOCK, axis=2)
        s = jnp.einsum('bqhd,bkhd->bhqk', qb, kp, preferred_element_type=jnp.float32) * scale
        s = s + db[..., :, None] - dk[:, :, None, :]
        q_idx = q0 + jnp.arange(BLOCK)
        mask = (key_idx[None, :] <= q_idx[:, None]) & (key_idx[None, :] >= PAD_FRONT)
        p = jax.nn.softmax(jnp.where(mask, s, NEG_INF), axis=-1)
        return jnp.einsum('bhqk,bkhd->bqhd', p.astype(vp.dtype), vp)

    out = lax.map(one_block, jnp.arange(Lp // BLOCK))
    out = out.transpose(1, 0, 2, 3, 4).reshape(B, Lp, N_HEADS, HEAD_DIM)
    return out[:, PAD_FRONT:]


def online_update(carry, s, vb):
    m, l, acc = carry
    m_new = jnp.maximum(m, jnp.max(s, axis=-1))
    corr = jnp.exp(m - m_new)
    p = jnp.exp(s - m_new[..., None])
    l = l * corr + jnp.sum(p, axis=-1)
    acc = acc * corr[..., None] + jnp.einsum('bhqk,bkhd->bhqd', p.astype(vb.dtype), vb,
                                             preferred_element_type=jnp.float32)
    return (m_new, l, acc)


def fox_attention_sample(q, k_new, v_new, logf_new, cache_k, cache_v, cache_logf, page_table, layer):
    Bd, T = q.shape[:2]
    n_pages = page_table.shape[1]
    past_logf = cache_logf[layer, page_table].astype(jnp.float32).reshape(Bd, n_pages * PAGE_SIZE, N_HEADS)
    after = lax.cumsum(past_logf, axis=1, reverse=True) - past_logf
    after_pages = after.reshape(Bd, n_pages, PAGE_SIZE, N_HEADS).transpose(1, 0, 3, 2)
    c_new = jnp.cumsum(logf_new, axis=1).transpose(0, 2, 1)
    scale = HEAD_DIM ** -0.5

    def step(carry, xs):
        pids, kbias = xs
        kb = cache_k[layer, pids]
        vb = cache_v[layer, pids]
        s = jnp.einsum('bqhd,bkhd->bhqk', q, kb, preferred_element_type=jnp.float32) * scale
        s = s + c_new[..., None] + kbias[:, :, None, :]
        return online_update(carry, s, vb), None

    init = (jnp.full((Bd, N_HEADS, T), -jnp.inf, jnp.float32),
            jnp.zeros((Bd, N_HEADS, T), jnp.float32),
            jnp.zeros((Bd, N_HEADS, T, HEAD_DIM), jnp.float32))
    carry, _ = lax.scan(step, init, (page_table.T, after_pages))
    s = jnp.einsum('bqhd,bkhd->bhqk', q, k_new, preferred_element_type=jnp.float32) * scale
    s = s + c_new[..., :, None] - c_new[..., None, :]
    causal = jnp.tril(jnp.ones((T, T), dtype=bool))
    m, l, acc = online_update(carry, jnp.where(causal, s, NEG_INF), v_new)
    return (acc / l[..., None]).transpose(0, 2, 1, 3).astype(q.dtype)


def setup_inputs(seed: int = 0) -> dict:
    key = jax.random.key(seed)
    ks = jax.random.split(key, 32)
    f32 = jnp.float32
    n_pages = PAST_LEN // PAGE_SIZE
    n_used = DEC_BATCH * n_pages
    n_pool = n_used + max(1, n_used // 4)
    nrm = lambda k, shp, s=1.0: s * jax.random.normal(k, shp, f32)
    w_in = nrm(ks[9], (DEPTH, D_MODEL, D_IN), D_MODEL ** -0.5)
    w_in = w_in.at[:, :, OFFS[2]:OFFS[3]].multiply(BETA)
    w_in = w_in.at[:, :, OFFS[3]:OFFS[4]].multiply(FORGET_W_SCALE)
    b_in = nrm(ks[10], (DEPTH, D_IN), 0.02).at[:, OFFS[3]:OFFS[4]].add(FORGET_BIAS)
    return {
        "x_prompt": nrm(ks[0], (BATCH, SEQ, D_MODEL)),
        "x_sample": nrm(ks[1], (DEC_BATCH, DEC_SEQ, D_MODEL)),
        "cache_k": nrm(ks[2], (DEPTH, n_pool, PAGE_SIZE, N_HEADS, HEAD_DIM)),
        "cache_v": nrm(ks[3], (DEPTH, n_pool, PAGE_SIZE, N_HEADS, HEAD_DIM)),
        "cache_logf": jax.nn.log_sigmoid(FORGET_BIAS + nrm(ks[4], (DEPTH, n_pool, PAGE_SIZE, N_HEADS), FORGET_W_SCALE)),
        "state_conv": nrm(ks[5], (DEPTH, DEC_BATCH, CONV_WIDTH - 1, D_CONV), 0.5),
        "page_table": jax.random.permutation(ks[6], n_pool)[:n_used].reshape(DEC_BATCH, n_pages).astype(jnp.int32),
        "meta_tokens": nrm(ks[7], (N_META, D_MODEL)),
        "w_ffn1_in": nrm(ks[8], (DEPTH, D_MODEL, 2 * D_FF), D_MODEL ** -0.5),
        "w_ffn1_out": nrm(ks[11], (DEPTH, D_FF, D_MODEL), BETA * D_FF ** -0.5),
        "ln1_g": 1.0 + nrm(ks[12], (DEPTH, D_MODEL), 0.02),
        "ln1_b": nrm(ks[13], (DEPTH, D_MODEL), 0.02),
        "w_in": w_in,
        "b_in": b_in,
        "conv_w": nrm(ks[14], (DEPTH, CONV_WIDTH, D_CONV), CONV_WIDTH ** -0.5),
        "conv_b": nrm(ks[15], (DEPTH, D_CONV), 0.02),
        "conv_ln_g": 1.0 + nrm(ks[16], (DEPTH, D_CONV), 0.02),
        "conv_ln_b": nrm(ks[17], (DEPTH, D_CONV), 0.02),
        "w_conv_out": nrm(ks[18], (DEPTH, D_CONV, D_MODEL), BETA * D_CONV ** -0.5),
        "w_att_proj": nrm(ks[19], (DEPTH, D_ATT, D_MODEL), BETA * D_ATT ** -0.5),
        "w_o": nrm(ks[20], (DEPTH, D_MODEL, D_MODEL), BETA * D_MODEL ** -0.5),
        "ln2_g": 1.0 + nrm(ks[21], (DEPTH, D_MODEL), 0.02),
        "ln2_b": nrm(ks[22], (DEPTH, D_MODEL), 0.02),
        "w_ffn2_in": nrm(ks[23], (DEPTH, D_MODEL, 2 * D_FF), D_MODEL ** -0.5),
        "w_ffn2_out": nrm(ks[24], (DEPTH, D_FF, D_MODEL), BETA * D_FF ** -0.5),
        "ln3_g": 1.0 + nrm(ks[25], (DEPTH, D_MODEL), 0.02),
        "ln3_b": nrm(ks[26], (DEPTH, D_MODEL), 0.02),
    }


def reference(x_prompt, x_sample, cache_k, cache_v, cache_logf, state_conv, page_table, meta_tokens,
              w_ffn1_in, w_ffn1_out, ln1_g, ln1_b, w_in, b_in, conv_w, conv_b, conv_ln_g, conv_ln_b,
              w_conv_out, w_att_proj, w_o, ln2_g, ln2_b, w_ffn2_in, w_ffn2_out, ln3_g, ln3_b):
    B = x_prompt.shape[0]
    meta = jnp.broadcast_to(meta_tokens.astype(x_prompt.dtype), (B, N_META, D_MODEL))
    xp = jnp.concatenate([meta, x_prompt], axis=1)
    xs = x_sample
    kp_l, vp_l, fp_l, cp_l, ks_l, vs_l, fs_l, cs_l = [], [], [], [], [], [], [], []
    for l in range(DEPTH):
        xp = ffn_sublayer(xp, w_ffn1_in[l], w_ffn1_out[l], ln1_g[l], ln1_b[l])
        xs = ffn_sublayer(xs, w_ffn1_in[l], w_ffn1_out[l], ln1_g[l], ln1_b[l])
        u, q, k, v, logf, gc, ga = branch_inputs(xp, w_in[l], b_in[l])
        u_hist = jnp.pad(u, ((0, 0), (CONV_WIDTH - 1, 0), (0, 0)))
        conv_d = conv_branch(u_hist, conv_w[l], conv_b[l], conv_ln_g[l], conv_ln_b[l], w_conv_out[l])
        att = fox_attention_prompt(q, k, v, logf)
        xp = layer_norm(ALPHA * xp + merge_branches(conv_d, att, gc, ga, w_att_proj[l], w_o[l]), ln2_g[l], ln2_b[l])
        kp_l.append(k); vp_l.append(v); fp_l.append(logf); cp_l.append(u_hist[:, -(CONV_WIDTH - 1):])
        u, q, k, v, logf, gc, ga = branch_inputs(xs, w_in[l], b_in[l])
        u_hist = jnp.concatenate([state_conv[l].astype(u.dtype), u], axis=1)
        conv_d = conv_branch(u_hist, conv_w[l], conv_b[l], conv_ln_g[l], conv_ln_b[l], w_conv_out[l])
        att = fox_attention_sample(q, k, v, logf, cache_k, cache_v, cache_logf, page_table, l)
        xs = layer_norm(ALPHA * xs + merge_branches(conv_d, att, gc, ga, w_att_proj[l], w_o[l]), ln2_g[l], ln2_b[l])
        ks_l.append(k); vs_l.append(v); fs_l.append(logf); cs_l.append(u_hist[:, -(CONV_WIDTH - 1):])
        xp = ffn_sublayer(xp, w_ffn2_in[l], w_ffn2_out[l], ln3_g[l], ln3_b[l])
        xs = ffn_sublayer(xs, w_ffn2_in[l], w_ffn2_out[l], ln3_g[l], ln3_b[l])
    y_prompt = xp[:, N_META:]
    y_sample = xs
    new_k_prompt = jnp.stack(kp_l, axis=0)
    new_v_prompt = jnp.stack(vp_l, axis=0)
    new_logf_prompt = jnp.stack(fp_l, axis=0)
    new_conv_prompt = jnp.stack(cp_l, axis=0)
    new_k_sample = jnp.stack(ks_l, axis=0)
    new_v_sample = jnp.stack(vs_l, axis=0)
    new_logf_sample = jnp.stack(fs_l, axis=0)
    new_conv_sample = jnp.stack(cs_l, axis=0)
    return (y_prompt, y_sample, new_k_prompt, new_v_prompt, new_logf_prompt, new_conv_prompt,
            new_k_sample, new_v_sample, new_logf_sample, new_conv_sample)
```

```python
import functools
import math

import jax
import jax.numpy as jnp
from jax import lax
from jax.experimental import pallas as pl
from jax.experimental.pallas import tpu as pltpu

F32 = jnp.float32
BF16 = jnp.bfloat16

N_META = 16
N_HEADS = 16
HEAD_DIM = 128
CONV_WIDTH = 31
PAGE_SIZE = 128
FORGET_SPLIT = N_HEADS
LN_EPS = 1e-5
NEG_INF = -1e30

V7X_VMEM_LIMIT_BYTES = 56 * 1024 * 1024
V7X_LANES = 128
V7X_SUBLANES = 8
NEW_ROWS = 16


def _cparams(semantics):
    return pltpu.CompilerParams(dimension_semantics=semantics, vmem_limit_bytes=V7X_VMEM_LIMIT_BYTES)


def _row_tile(rows, preferred):
    best = None
    for t in range(16, min(rows, preferred) + 1, 16):
        if rows % t == 0:
            best = t
    return best if best is not None else rows


def _layer_norm(y, g, b):
    mu = jnp.mean(y, axis=-1, keepdims=True)
    d = y - mu
    var = jnp.mean(d * d, axis=-1, keepdims=True)
    return d * lax.rsqrt(var + LN_EPS) * g + b


def _log_sigmoid(z):
    return jnp.minimum(z, 0.0) - jnp.log1p(jnp.exp(-jnp.abs(z)))


def _split3(x):
    hi = x.astype(BF16)
    r1 = x - hi.astype(F32)
    mid = r1.astype(BF16)
    lo = (r1 - mid.astype(F32)).astype(BF16)
    return hi, mid, lo


def _dot01(sel, x):
    hi, mid, lo = _split3(x)
    out = jnp.dot(sel, lo, preferred_element_type=F32)
    out = out + jnp.dot(sel, mid, preferred_element_type=F32)
    return out + jnp.dot(sel, hi, preferred_element_type=F32)


def _ffn_kernel(x_ref, wa_ref, wb_ref, wo_ref, g_ref, b_ref, *rest, alpha, n_f, emit_bf16):
    if emit_bf16:
        o_ref, ob_ref, xb_ref, acc_ref = rest
    else:
        o_ref, xb_ref, acc_ref = rest
    f = pl.program_id(1)

    @pl.when(f == 0)
    def _init():
        xb_ref[...] = x_ref[...].astype(BF16)
        acc_ref[...] = jnp.zeros_like(acc_ref)

    xb = xb_ref[...]
    a = jnp.dot(xb, wa_ref[...], preferred_element_type=F32)
    b = jnp.dot(xb, wb_ref[...], preferred_element_type=F32)
    h = (a * jax.nn.sigmoid(a) * b).astype(BF16)
    acc_ref[...] += jnp.dot(h, wo_ref[...], preferred_element_type=F32)

    @pl.when(f == n_f - 1)
    def _finish():
        y = alpha * x_ref[...] + 0.5 * acc_ref[...]
        o = _layer_norm(y, g_ref[...], b_ref[...])
        o_ref[...] = o
        if emit_bf16:
            ob_ref[...] = o.astype(BF16)


def _ffn(x, w_in, w_out, g, b, *, alpha, emit_bf16, tm_pref=512, tf=512):
    rows, d = x.shape
    d_ff = w_out.shape[0]
    tm = _row_tile(rows, tm_pref)
    n_f = d_ff // tf
    out_shape = [jax.ShapeDtypeStruct((rows, d), F32)]
    out_specs = [pl.BlockSpec((tm, d), lambda i, f: (i, 0))]
    if emit_bf16:
        out_shape.append(jax.ShapeDtypeStruct((rows, d), BF16))
        out_specs.append(pl.BlockSpec((tm, d), lambda i, f: (i, 0)))
    res = pl.pallas_call(
        functools.partial(_ffn_kernel, alpha=alpha, n_f=n_f, emit_bf16=emit_bf16),
        grid=(rows // tm, n_f),
        in_specs=[
            pl.BlockSpec((tm, d), lambda i, f: (i, 0)),
            pl.BlockSpec((d, tf), lambda i, f: (0, f)),
            pl.BlockSpec((d, tf), lambda i, f: (0, f + n_f)),
            pl.BlockSpec((tf, d), lambda i, f: (f, 0)),
            pl.BlockSpec((1, d), lambda i, f: (0, 0)),
            pl.BlockSpec((1, d), lambda i, f: (0, 0)),
        ],
        out_specs=out_specs,
        out_shape=out_shape,
        scratch_shapes=[pltpu.VMEM((tm, d), BF16), pltpu.VMEM((tm, d), F32)],
        compiler_params=_cparams(("parallel", "arbitrary")),
        name="ffn_sublayer",
    )(x, w_in, w_in, w_out, g, b)
    return res if emit_bf16 else res[0]


def _proj_kernel(x_ref, *refs, mode, n_prefix):
    x = x_ref[0]
    if mode == "glu":
        wa_ref, wg_ref, ba_ref, bg_ref = refs[:4]
        refs = refs[4:]
        za = jnp.dot(x, wa_ref[...], preferred_element_type=F32) + ba_ref[...]
        zg = jnp.dot(x, wg_ref[...], preferred_element_type=F32) + bg_ref[...]
        z = za * jax.nn.sigmoid(zg)
    else:
        w_ref, b_ref = refs[:2]
        refs = refs[2:]
        z = jnp.dot(x, w_ref[...], preferred_element_type=F32) + b_ref[...]
        if mode == "log_sigmoid":
            z = _log_sigmoid(z)
    if n_prefix:
        pre_ref, o_ref = refs
        o_ref[0, :n_prefix, :] = pre_ref[...].astype(o_ref.dtype)
        o_ref[0, n_prefix:, :] = z.astype(o_ref.dtype)
    else:
        (o_ref,) = refs
        o_ref[0] = z.astype(o_ref.dtype)


def _proj(x, w, bias, *, mode, out_dtype, prefix=None, tn=512):
    bsz, rows, d = x.shape
    n = w.shape[1] // 2 if mode == "glu" else w.shape[1]
    tn = min(tn, n)
    n_t = n // tn
    n_prefix = 0 if prefix is None else prefix.shape[0]
    in_specs = [pl.BlockSpec((1, rows, d), lambda b, j: (b, 0, 0))]
    args = [x]
    if mode == "glu":
        in_specs += [
            pl.BlockSpec((d, tn), lambda b, j: (0, j)),
            pl.BlockSpec((d, tn), lambda b, j: (0, j + n_t)),
            pl.BlockSpec((1, tn), lambda b, j: (0, j)),
            pl.BlockSpec((1, tn), lambda b, j: (0, j + n_t)),
        ]
        args += [w, w, bias, bias]
    else:
        in_specs += [pl.BlockSpec((d, tn), lambda b, j: (0, j)), pl.BlockSpec((1, tn), lambda b, j: (0, j))]
        args += [w, bias]
    if n_prefix:
        in_specs.append(pl.BlockSpec((n_prefix, tn), lambda b, j: (0, j)))
        args.append(prefix)
    return pl.pallas_call(
        functools.partial(_proj_kernel, mode=mode, n_prefix=n_prefix),
        grid=(bsz, n_t),
        in_specs=in_specs,
        out_specs=pl.BlockSpec((1, n_prefix + rows, tn), lambda b, j: (b, 0, j)),
        out_shape=jax.ShapeDtypeStruct((bsz, n_prefix + rows, n), out_dtype),
        compiler_params=_cparams(("parallel", "arbitrary")),
        name="in_proj_" + mode,
    )(*args)


def _cumsum_kernel(x_ref, o_ref, *, chunk):
    length, lanes = x_ref.shape
    r = lax.broadcasted_iota(jnp.int32, (chunk, chunk), 0)
    c = lax.broadcasted_iota(jnp.int32, (chunk, chunk), 1)
    tri = jnp.where(c <= r, 1.0, 0.0).astype(BF16)
    carry = jnp.zeros((1, lanes), F32)
    for start in range(0, length, chunk):
        size = min(chunk, length - start)
        local = _dot01(tri[:size, :size], x_ref[start:start + size, :])
        o_ref[start:start + size, :] = local + carry
        carry = carry + local[size - 1:size, :]


def _cumsum_rows(x, chunk=256):
    return pl.pallas_call(
        functools.partial(_cumsum_kernel, chunk=chunk),
        out_shape=jax.ShapeDtypeStruct(x.shape, F32),
        name="logf_cumsum",
    )(x)


def _attn_prompt_kernel(q_ref, k_ref, v_ref, dkm_ref, dkr_ref, o_ref, *, tq, scale):
    seq = q_ref.shape[1]
    kb = k_ref[0].astype(BF16)
    vb = v_ref[0].astype(BF16)
    k_meta, v_meta = kb[:N_META], vb[:N_META]
    bias_meta = -dkm_ref[0, 0]
    contract_last = (((1,), (1,)), ((), ()))
    for qi in range(seq // tq):
        n_keys = (qi + 1) * tq
        q = q_ref[0, qi * tq:(qi + 1) * tq, :]
        k_real = kb[N_META:N_META + n_keys]
        v_real = vb[N_META:N_META + n_keys]
        s = lax.dot_general(q, k_real, contract_last, preferred_element_type=F32) * scale
        s = s - dkr_ref[0, 0, :, :n_keys]
        row = qi * tq + lax.broadcasted_iota(jnp.int32, (tq, n_keys), 0)
        col = lax.broadcasted_iota(jnp.int32, (tq, n_keys), 1)
        s = jnp.where(col <= row, s, NEG_INF)
        s_meta = lax.dot_general(q, k_meta, contract_last, preferred_element_type=F32) * scale + bias_meta
        m = jnp.maximum(jnp.max(s, axis=-1, keepdims=True), jnp.max(s_meta, axis=-1, keepdims=True))
        p = jnp.exp(s - m)
        p_meta = jnp.exp(s_meta - m)
        denom = jnp.sum(p, axis=-1, keepdims=True) + jnp.sum(p_meta, axis=-1, keepdims=True)
        o = jnp.dot(p.astype(BF16), v_real, preferred_element_type=F32)
        o = o + jnp.dot(p_meta.astype(BF16), v_meta, preferred_element_type=F32)
        o_ref[0, qi * tq:(qi + 1) * tq, :] = (o / denom).astype(o_ref.dtype)


def _attn_prompt(q, k, v, dk_meta, dk_real, *, tq_pref=256):
    bsz, seq, _ = q.shape
    length = k.shape[1]
    tq = _row_tile(seq, tq_pref)
    return pl.pallas_call(
        functools.partial(_attn_prompt_kernel, tq=tq, scale=HEAD_DIM ** -0.5),
        grid=(bsz, N_HEADS),
        in_specs=[
            pl.BlockSpec((1, seq, HEAD_DIM), lambda b, h: (b, 0, h)),
            pl.BlockSpec((1, length, HEAD_DIM), lambda b, h: (b, 0, h)),
            pl.BlockSpec((1, length, HEAD_DIM), lambda b, h: (b, 0, h)),
            pl.BlockSpec((1, 1, 1, N_META), lambda b, h: (b, h, 0, 0)),
            pl.BlockSpec((1, 1, 1, seq), lambda b, h: (b, h, 0, 0)),
        ],
        out_specs=pl.BlockSpec((1, seq, HEAD_DIM), lambda b, h: (b, 0, h)),
        out_shape=jax.ShapeDtypeStruct(q.shape, BF16),
        compiler_params=_cparams(("parallel", "arbitrary")),
        name="attn_prompt",
    )(q, k, v, dk_meta, dk_real)


def _conv_kernel(u_ref, cw_ref, cb_ref, g_ref, b_ref, o_ref, h_ref, *, tr, off, sub_rows):
    channels = o_ref.shape[2]
    base = pl.multiple_of(pl.program_id(1) * tr, V7X_SUBLANES) if tr % V7X_SUBLANES == 0 else 0
    window = u_ref.at[0, pl.ds(base, tr + off + CONV_WIDTH - 1), :]
    for c0 in range(0, channels, V7X_LANES):
        for r0 in range(0, tr, sub_rows):
            acc = jnp.zeros((sub_rows, V7X_LANES), F32)
            for w in range(CONV_WIDTH):
                rows = window[r0 + off + w:r0 + off + w + sub_rows, c0:c0 + V7X_LANES]
                acc = acc + cw_ref[w:w + 1, c0:c0 + V7X_LANES] * rows
            h_ref[r0:r0 + sub_rows, c0:c0 + V7X_LANES] = acc
    hn = _layer_norm(h_ref[...] + cb_ref[...], g_ref[...], b_ref[...])
    o_ref[0] = (hn * jax.nn.sigmoid(hn)).astype(o_ref.dtype)


def _conv_branch(hist, conv_w, conv_b, ln_g, ln_b, *, n_out, off, tr_pref=256):
    bsz, hist_rows, channels = hist.shape
    tr = _row_tile(n_out, tr_pref) if n_out % 16 == 0 else n_out
    assert hist_rows >= n_out + off + CONV_WIDTH - 1
    sub_rows = math.gcd(tr, 64)
    return pl.pallas_call(
        functools.partial(_conv_kernel, tr=tr, off=off, sub_rows=sub_rows),
        grid=(bsz, n_out // tr),
        in_specs=[
            pl.BlockSpec((1, hist_rows, channels), lambda b, r: (b, 0, 0)),
            pl.BlockSpec((CONV_WIDTH, channels), lambda b, r: (0, 0)),
            pl.BlockSpec((1, channels), lambda b, r: (0, 0)),
            pl.BlockSpec((1, channels), lambda b, r: (0, 0)),
            pl.BlockSpec((1, channels), lambda b, r: (0, 0)),
        ],
        out_specs=pl.BlockSpec((1, tr, channels), lambda b, r: (b, r, 0)),
        out_shape=jax.ShapeDtypeStruct((bsz, n_out, channels), BF16),
        scratch_shapes=[pltpu.VMEM((tr, channels), F32)],
        compiler_params=_cparams(("parallel", "arbitrary")),
        name="conv_branch",
    )(hist, conv_w, conv_b, ln_g, ln_b)


def _merge_kernel(x_ref, hc_ref, att_ref, wgc_ref, wga_ref, wco_ref, wap_ref, bgc_ref, bga_ref, o_ref):
    x = x_ref[...]
    g_conv = jax.nn.sigmoid(jnp.dot(x, wgc_ref[...], preferred_element_type=F32) + bgc_ref[...])
    g_att = jax.nn.sigmoid(jnp.dot(x, wga_ref[...], preferred_element_type=F32) + bga_ref[...])
    conv_d = jnp.dot(hc_ref[...], wco_ref[...], preferred_element_type=F32)
    att_d = jnp.dot(att_ref[...], wap_ref[...], preferred_element_type=F32)
    o_ref[...] = (g_conv * conv_d + g_att * att_d).astype(o_ref.dtype)


def _merge(xb, h_conv, att, w_gc, w_ga, w_co, w_ap, b_gc, b_ga, *, tm_pref=1024, tn=512):
    rows, d = xb.shape
    d_conv = h_conv.shape[1]
    d_att = att.shape[1]
    tm = _row_tile(rows, tm_pref)
    return pl.pallas_call(
        _merge_kernel,
        grid=(rows // tm, d // tn),
        in_specs=[
            pl.BlockSpec((tm, d), lambda i, j: (i, 0)),
            pl.BlockSpec((tm, d_conv), lambda i, j: (i, 0)),
            pl.BlockSpec((tm, d_att), lambda i, j: (i, 0)),
            pl.BlockSpec((d, tn), lambda i, j: (0, j)),
            pl.BlockSpec((d, tn), lambda i, j: (0, j)),
            pl.BlockSpec((d_conv, tn), lambda i, j: (0, j)),
            pl.BlockSpec((d_att, tn), lambda i, j: (0, j)),
            pl.BlockSpec((1, tn), lambda i, j: (0, j)),
            pl.BlockSpec((1, tn), lambda i, j: (0, j)),
        ],
        out_specs=pl.BlockSpec((tm, tn), lambda i, j: (i, j)),
        out_shape=jax.ShapeDtypeStruct((rows, d), BF16),
        compiler_params=_cparams(("parallel", "arbitrary")),
        name="branch_merge",
    )(xb, h_conv, att, w_gc, w_ga, w_co, w_ap, b_gc, b_ga)


def _mix_out_kernel(x_ref, m_ref, wo_ref, g_ref, b_ref, o_ref, *, alpha):
    y = alpha * x_ref[...] + jnp.dot(m_ref[...], wo_ref[...], preferred_element_type=F32)
    o_ref[...] = _layer_norm(y, g_ref[...], b_ref[...])


def _mix_out(x, m, w_o, g, b, *, alpha, tm_pref=512):
    rows, d = x.shape
    tm = _row_tile(rows, tm_pref)
    return pl.pallas_call(
        functools.partial(_mix_out_kernel, alpha=alpha),
        grid=(rows // tm,),
        in_specs=[
            pl.BlockSpec((tm, d), lambda i: (i, 0)),
            pl.BlockSpec((tm, d), lambda i: (i, 0)),
            pl.BlockSpec((d, d), lambda i: (0, 0)),
            pl.BlockSpec((1, d), lambda i: (0, 0)),
            pl.BlockSpec((1, d), lambda i: (0, 0)),
        ],
        out_specs=pl.BlockSpec((tm, d), lambda i: (i, 0)),
        out_shape=jax.ShapeDtypeStruct((rows, d), F32),
        compiler_params=_cparams(("parallel",)),
        name="mixer_out",
    )(x, m, w_o, g, b)


def _attn_sample_kernel(pt_ref, qt_ref, kn_ref, vn_ref, fn_ref, *refs, pages_per_step, n_tok, scale):
    pp = pages_per_step
    k_refs, v_refs, f_refs = refs[:pp], refs[pp:2 * pp], refs[2 * pp:3 * pp]
    o_ref, m_ref, l_ref, acc_ref, carry_ref = refs[3 * pp:]
    del pt_ref
    j = pl.program_id(1)
    n_rows = n_tok * N_HEADS
    d_att = N_HEADS * HEAD_DIM
    qt = qt_ref[0]

    eh = lax.broadcasted_iota(jnp.int32, (N_HEADS, V7X_LANES), 0)
    ec = lax.broadcasted_iota(jnp.int32, (N_HEADS, V7X_LANES), 1)
    expand = jnp.where((ec % N_HEADS == eh) & (ec < n_rows), 1.0, 0.0).astype(BF16)

    def expand_heads(x):
        hi, mid, lo = _split3(x)
        out = jnp.dot(lo, expand, preferred_element_type=F32)
        out = out + jnp.dot(mid, expand, preferred_element_type=F32)
        return out + jnp.dot(hi, expand, preferred_element_type=F32)

    def online_update(s, v_bf16):
        m_old = m_ref[...]
        m_new = jnp.maximum(m_old, jnp.max(s, axis=-1, keepdims=True))
        corr = jnp.exp(m_old - m_new)
        p = jnp.exp(s - m_new)
        l_ref[...] = l_ref[...] * corr + jnp.sum(p, axis=-1, keepdims=True)
        acc_ref[...] = acc_ref[...] * corr + jnp.dot(p.astype(BF16), v_bf16, preferred_element_type=F32)
        m_ref[...] = m_new

    @pl.when(j == 0)
    def _new_tokens():
        f_new = fn_ref[0]
        r = lax.broadcasted_iota(jnp.int32, (NEW_ROWS, NEW_ROWS), 0)
        c = lax.broadcasted_iota(jnp.int32, (NEW_ROWS, NEW_ROWS), 1)
        tri = jnp.where(c <= r, 1.0, 0.0).astype(BF16)
        c_new = _dot01(tri, f_new)
        s_t = jnp.dot(kn_ref[0], qt, preferred_element_type=F32) * scale - expand_heads(c_new)
        key = lax.broadcasted_iota(jnp.int32, (NEW_ROWS, V7X_LANES), 0)
        tok = lax.broadcasted_iota(jnp.int32, (NEW_ROWS, V7X_LANES), 1) // N_HEADS
        s_t = jnp.where((key <= tok) & (key < n_tok), s_t, NEG_INF)
        s = jnp.transpose(jnp.concatenate([s_t] * (V7X_LANES // NEW_ROWS), axis=0))[:n_rows, :NEW_ROWS]
        m_ref[...] = jnp.full_like(m_ref, -jnp.inf)
        l_ref[...] = jnp.zeros_like(l_ref)
        acc_ref[...] = jnp.zeros_like(acc_ref)
        carry_ref[...] = jnp.zeros_like(carry_ref)
        online_update(s, vn_ref[0])

    r = lax.broadcasted_iota(jnp.int32, (PAGE_SIZE, PAGE_SIZE), 0)
    c = lax.broadcasted_iota(jnp.int32, (PAGE_SIZE, PAGE_SIZE), 1)
    later = jnp.where(c > r, 1.0, 0.0).astype(BF16)
    for i in range(pp):
        logf = expand_heads(f_refs[i][0])
        after = _dot01(later, logf) + carry_ref[...]
        carry_ref[...] = carry_ref[...] + jnp.sum(logf, axis=0, keepdims=True)
        s_t = jnp.dot(k_refs[i][0].astype(BF16), qt, preferred_element_type=F32) * scale + after
        s = jnp.transpose(s_t)[:n_rows]
        online_update(s, v_refs[i][0].astype(BF16))

    @pl.when(j == pl.num_programs(1) - 1)
    def _finish():
        o = acc_ref[...] / l_ref[...]
        row_head = lax.broadcasted_iota(jnp.int32, (n_rows, d_att), 0) % N_HEADS
        col_head = lax.broadcasted_iota(jnp.int32, (n_rows, d_att), 1) // HEAD_DIM
        o = jnp.where(row_head == col_head, o, 0.0)
        o_ref[0] = jnp.sum(o.reshape(n_tok, N_HEADS, d_att), axis=1).astype(o_ref.dtype)


def _attn_sample(q, k_new, v_new, logf_new, cache_k, cache_v, cache_logf, page_table, *, pages_per_step=4):
    bd, n_tok, d_att = q.shape
    n_pages = page_table.shape[1]
    pp = pages_per_step
    while n_pages % pp:
        pp -= 1
    n_rows = n_tok * N_HEADS
    assert n_rows <= V7X_LANES and n_tok <= NEW_ROWS
    q4 = q.reshape(bd, n_tok, N_HEADS, HEAD_DIM).astype(F32)
    eye = jnp.eye(N_HEADS, dtype=F32)
    qt = jnp.einsum("bthd,hg->bhdtg", q4, eye).reshape(bd, d_att, n_rows)
    qt = jnp.pad(qt, ((0, 0), (0, 0), (0, V7X_LANES - n_rows))).astype(BF16)
    pad_t = ((0, 0), (0, NEW_ROWS - n_tok), (0, 0))
    kn = jnp.pad(k_new, pad_t).astype(BF16)
    vn = jnp.pad(v_new, pad_t).astype(BF16)
    fn = jnp.pad(logf_new, pad_t)

    def page_map(i):
        return lambda b, j, pt: (pt[b, n_pages - 1 - (j * pp + i)], 0, 0)

    kv_specs = [pl.BlockSpec((1, PAGE_SIZE, d_att), page_map(i)) for i in range(pp)]
    f_specs = [pl.BlockSpec((1, PAGE_SIZE, N_HEADS), page_map(i)) for i in range(pp)]
    per_batch = lambda b, j, pt: (b, 0, 0)
    grid_spec = pltpu.PrefetchScalarGridSpec(
        num_scalar_prefetch=1,
        grid=(bd, n_pages // pp),
        in_specs=[
            pl.BlockSpec((1, d_att, V7X_LANES), per_batch),
            pl.BlockSpec((1, NEW_ROWS, d_att), per_batch),
            pl.BlockSpec((1, NEW_ROWS, d_att), per_batch),
            pl.BlockSpec((1, NEW_ROWS, N_HEADS), per_batch),
        ] + kv_specs + kv_specs + f_specs,
        out_specs=pl.BlockSpec((1, n_tok, d_att), per_batch),
        scratch_shapes=[
            pltpu.VMEM((n_rows, 1), F32),
            pltpu.VMEM((n_rows, 1), F32),
            pltpu.VMEM((n_rows, d_att), F32),
            pltpu.VMEM((1, V7X_LANES), F32),
        ],
    )
    return pl.pallas_call(
        functools.partial(_attn_sample_kernel, pages_per_step=pp, n_tok=n_tok, scale=HEAD_DIM ** -0.5),
        grid_spec=grid_spec,
        out_shape=jax.ShapeDtypeStruct((bd, n_tok, d_att), BF16),
        compiler_params=_cparams(("parallel", "arbitrary")),
        name="attn_sample",
    )(page_table, qt, kn, vn, fn, *([cache_k] * pp), *([cache_v] * pp), *([cache_logf] * pp))


def kernel(x_prompt, x_sample, cache_k, cache_v, cache_logf, state_conv, page_table, meta_tokens,
           w_ffn1_in, w_ffn1_out, ln1_g, ln1_b, w_in, b_in, conv_w, conv_b, conv_ln_g, conv_ln_b,
           w_conv_out, w_att_proj, w_o, ln2_g, ln2_b, w_ffn2_in, w_ffn2_out, ln3_g, ln3_b):
    depth = w_ffn1_in.shape[0]
    assert depth == 1, "the meta rows are carried through a single layer only"
    bsz, seq, d = x_prompt.shape
    bd, n_tok, _ = x_sample.shape
    d_att = N_HEADS * HEAD_DIM
    d_conv = conv_w.shape[2]
    alpha = (2.0 * depth) ** 0.25
    n_pool = cache_k.shape[1]
    o_q = 2 * d_conv
    o_k, o_v, o_f = o_q + d_att, o_q + 2 * d_att, o_q + 3 * d_att
    o_gc = o_f + FORGET_SPLIT
    o_ga = o_gc + d
    row = lambda a: a.reshape(1, -1)

    xp = x_prompt.reshape(bsz * seq, d)
    xs = jnp.concatenate([meta_tokens.astype(F32), x_sample.reshape(bd * n_tok, d)], axis=0)
    outs = [[] for _ in range(8)]
    for l in range(depth):
        w1i, w1o = w_ffn1_in[l].astype(BF16), w_ffn1_out[l].astype(BF16)
        w2i, w2o = w_ffn2_in[l].astype(BF16), w_ffn2_out[l].astype(BF16)
        wl, bl = w_in[l], b_in[l]
        w_u, b_u = wl[:, :o_q].astype(BF16), row(bl[:o_q])
        w_q, b_q = wl[:, o_q:o_k].astype(BF16), row(bl[o_q:o_k])
        w_k, b_k = wl[:, o_k:o_v].astype(BF16), row(bl[o_k:o_v])
        w_v, b_v = wl[:, o_v:o_f].astype(BF16), row(bl[o_v:o_f])
        w_f, b_f = wl[:, o_f:o_gc].astype(BF16), row(bl[o_f:o_gc])
        w_gc, b_gc = wl[:, o_gc:o_ga].astype(BF16), row(bl[o_gc:o_ga])
        w_ga, b_ga = wl[:, o_ga:].astype(BF16), row(bl[o_ga:])
        w_co, w_ap, w_oo = w_conv_out[l].astype(BF16), w_att_proj[l].astype(BF16), w_o[l].astype(BF16)
        g1, b1, g2, b2, g3, b3 = row(ln1_g[l]), row(ln1_b[l]), row(ln2_g[l]), row(ln2_b[l]), row(ln3_g[l]), row(ln3_b[l])
        cb, cg, cbb = row(conv_b[l]), row(conv_ln_g[l]), row(conv_ln_b[l])

        xp, xp_b = _ffn(xp, w1i, w1o, g1, b1, alpha=alpha, emit_bf16=True)
        xs, xs_b = _ffn(xs, w1i, w1o, g1, b1, alpha=alpha, emit_bf16=True)

        xs_b3 = xs_b[None]
        u_s = _proj(xs_b3, w_u, b_u, mode="glu", out_dtype=F32)[0]
        q_s = _proj(xs_b3, w_q, b_q, mode="linear", out_dtype=BF16)[0]
        k_s = _proj(xs_b3, w_k, b_k, mode="linear", out_dtype=F32)[0]
        v_s = _proj(xs_b3, w_v, b_v, mode="linear", out_dtype=F32)[0]
        f_s = _proj(xs_b3, w_f, b_f, mode="log_sigmoid", out_dtype=F32)[0]

        xp_b3 = xp_b.reshape(bsz, seq, d)
        u_pre = jnp.concatenate([jnp.zeros((N_META, d_conv), F32), u_s[:N_META]], axis=0)
        u_p = _proj(xp_b3, w_u, b_u, mode="glu", out_dtype=F32, prefix=u_pre)
        q_p = _proj(xp_b3, w_q, b_q, mode="linear", out_dtype=BF16)
        k_p = _proj(xp_b3, w_k, b_k, mode="linear", out_dtype=F32, prefix=k_s[:N_META])
        v_p = _proj(xp_b3, w_v, b_v, mode="linear", out_dtype=F32, prefix=v_s[:N_META])
        f_p = _proj(xp_b3, w_f, b_f, mode="log_sigmoid", out_dtype=F32, prefix=f_s[:N_META])

        length = N_META + seq
        f_lanes = f_p.transpose(1, 0, 2).reshape(length, bsz * N_HEADS)
        dk = _cumsum_rows(f_lanes).reshape(length, bsz, N_HEADS).transpose(1, 2, 0)
        dk_meta = dk[:, :, None, :N_META]
        dk_real = dk[:, :, None, N_META:]
        att_p = _attn_prompt(q_p, k_p, v_p, dk_meta, dk_real)
        hc_p = _conv_branch(u_p, conv_w[l], cb, cg, cbb, n_out=seq, off=2)
        m_p = _merge(xp_b, hc_p.reshape(bsz * seq, d_conv), att_p.reshape(bsz * seq, d_att),
                     w_gc, w_ga, w_co, w_ap, b_gc, b_ga)
        xp = _mix_out(xp, m_p, w_oo, g2, b2, alpha=alpha)

        sl = slice(N_META, None)
        u_new = u_s[sl].reshape(bd, n_tok, d_conv)
        hist = jnp.concatenate([state_conv[l].astype(F32), u_new], axis=1)
        hc_s = _conv_branch(hist, conv_w[l], cb, cg, cbb, n_out=n_tok, off=0)
        att_s = _attn_sample(
            q_s[sl].reshape(bd, n_tok, d_att), k_s[sl].reshape(bd, n_tok, d_att), v_s[sl].reshape(bd, n_tok, d_att),
            f_s[sl].reshape(bd, n_tok, N_HEADS),
            cache_k[l].reshape(n_pool, PAGE_SIZE, d_att), cache_v[l].reshape(n_pool, PAGE_SIZE, d_att),
            cache_logf[l], page_table)
        xs_s = xs[sl]
        m_s = _merge(xs_b[sl], hc_s.reshape(bd * n_tok, d_conv), att_s.reshape(bd * n_tok, d_att),
                     w_gc, w_ga, w_co, w_ap, b_gc, b_ga)
        xs_s = _mix_out(xs_s, m_s, w_oo, g2, b2, alpha=alpha)

        xp = _ffn(xp, w2i, w2o, g3, b3, alpha=alpha, emit_bf16=False)
        xs_s = _ffn(xs_s, w2i, w2o, g3, b3, alpha=alpha, emit_bf16=False)
        xs = jnp.concatenate([meta_tokens.astype(F32), xs_s], axis=0)

        hd_shape = (N_HEADS, HEAD_DIM)
        outs[0].append(k_p.reshape(bsz, length, *hd_shape))
        outs[1].append(v_p.reshape(bsz, length, *hd_shape))
        outs[2].append(f_p)
        outs[3].append(u_p[:, -(CONV_WIDTH - 1):])
        outs[4].append(k_s[sl].reshape(bd, n_tok, *hd_shape))
        outs[5].append(v_s[sl].reshape(bd, n_tok, *hd_shape))
        outs[6].append(f_s[sl].reshape(bd, n_tok, N_HEADS))
        outs[7].append(hist[:, -(CONV_WIDTH - 1):])

    y_prompt = xp.reshape(bsz, seq, d)
    y_sample = xs[N_META:].reshape(bd, n_tok, d)
    return (y_prompt, y_sample) + tuple(jnp.stack(o, axis=0) for o in outs)
```

```python
import functools
import math

import jax
import jax.numpy as jnp
from jax import lax
from jax.experimental import pallas as pl
from jax.experimental.pallas import tpu as pltpu

F32 = jnp.float32
BF16 = jnp.bfloat16

N_META = 16
N_HEADS = 16
HEAD_DIM = 128
CONV_WIDTH = 31
PAGE_SIZE = 128
FORGET_SPLIT = N_HEADS
LN_EPS = 1e-5
NEG_INF = -1e30

V7X_VMEM_LIMIT_BYTES = 56 * 1024 * 1024
V7X_LANES = 128
V7X_SUBLANES = 8
NEW_TOKENS = 8
CUMSUM_ROW_ALIGN = 16


def _cparams(semantics):
    return pltpu.CompilerParams(dimension_semantics=semantics, vmem_limit_bytes=V7X_VMEM_LIMIT_BYTES)


def _row_tile(rows, preferred):
    best = None
    for t in range(16, min(rows, preferred) + 1, 16):
        if rows % t == 0:
            best = t
    return best if best is not None else rows


def _layer_norm(y, g, b):
    mu = jnp.mean(y, axis=-1, keepdims=True)
    d = y - mu
    var = jnp.mean(d * d, axis=-1, keepdims=True)
    return d * lax.rsqrt(var + LN_EPS) * g + b


def _log_sigmoid(z):
    return jnp.minimum(z, 0.0) - jnp.log1p(jnp.exp(-jnp.abs(z)))


def _split3(x):
    hi = x.astype(BF16)
    r1 = x - hi.astype(F32)
    mid = r1.astype(BF16)
    lo = (r1 - mid.astype(F32)).astype(BF16)
    return hi, mid, lo


def _dot01(sel, x):
    hi, mid, lo = _split3(x)
    out = jnp.dot(sel, lo, preferred_element_type=F32)
    out = out + jnp.dot(sel, mid, preferred_element_type=F32)
    return out + jnp.dot(sel, hi, preferred_element_type=F32)


def _ffn_kernel(x_ref, wa_ref, wb_ref, wo_ref, g_ref, b_ref, *rest, alpha, n_f, emit_bf16):
    if emit_bf16:
        o_ref, ob_ref, xb_ref, acc_ref = rest
    else:
        o_ref, xb_ref, acc_ref = rest
    f = pl.program_id(1)

    @pl.when(f == 0)
    def _init():
        xb_ref[...] = x_ref[...].astype(BF16)
        acc_ref[...] = jnp.zeros_like(acc_ref)

    xb = xb_ref[...]
    a = jnp.dot(xb, wa_ref[...], preferred_element_type=F32)
    b = jnp.dot(xb, wb_ref[...], preferred_element_type=F32)
    h = (a * jax.nn.sigmoid(a) * b).astype(BF16)
    acc_ref[...] += jnp.dot(h, wo_ref[...], preferred_element_type=F32)

    @pl.when(f == n_f - 1)
    def _finish():
        y = alpha * x_ref[...] + 0.5 * acc_ref[...]
        o = _layer_norm(y, g_ref[...], b_ref[...])
        o_ref[...] = o
        if emit_bf16:
            ob_ref[...] = o.astype(BF16)


def _ffn(x, w_in, w_out, g, b, *, alpha, emit_bf16, tm_pref=512, tf=512):
    rows, d = x.shape
    d_ff = w_out.shape[0]
    tm = _row_tile(rows, tm_pref)
    n_f = d_ff // tf
    out_shape = [jax.ShapeDtypeStruct((rows, d), F32)]
    out_specs = [pl.BlockSpec((tm, d), lambda i, f: (i, 0))]
    if emit_bf16:
        out_shape.append(jax.ShapeDtypeStruct((rows, d), BF16))
        out_specs.append(pl.BlockSpec((tm, d), lambda i, f: (i, 0)))
    res = pl.pallas_call(
        functools.partial(_ffn_kernel, alpha=alpha, n_f=n_f, emit_bf16=emit_bf16),
        grid=(rows // tm, n_f),
        in_specs=[
            pl.BlockSpec((tm, d), lambda i, f: (i, 0)),
            pl.BlockSpec((d, tf), lambda i, f: (0, f)),
            pl.BlockSpec((d, tf), lambda i, f: (0, f + n_f)),
            pl.BlockSpec((tf, d), lambda i, f: (f, 0)),
            pl.BlockSpec((1, d), lambda i, f: (0, 0)),
            pl.BlockSpec((1, d), lambda i, f: (0, 0)),
        ],
        out_specs=out_specs,
        out_shape=out_shape,
        scratch_shapes=[pltpu.VMEM((tm, d), BF16), pltpu.VMEM((tm, d), F32)],
        compiler_params=_cparams(("parallel", "arbitrary")),
        name="ffn_sublayer",
    )(x, w_in, w_in, w_out, g, b)
    return res if emit_bf16 else res[0]


def _proj_kernel(x_ref, *refs, mode, n_prefix):
    x = x_ref[0]
    if mode == "glu":
        wa_ref, wg_ref, ba_ref, bg_ref = refs[:4]
        refs = refs[4:]
        za = jnp.dot(x, wa_ref[...], preferred_element_type=F32) + ba_ref[...]
        zg = jnp.dot(x, wg_ref[...], preferred_element_type=F32) + bg_ref[...]
        z = za * jax.nn.sigmoid(zg)
    else:
        w_ref, b_ref = refs[:2]
        refs = refs[2:]
        z = jnp.dot(x, w_ref[...], preferred_element_type=F32) + b_ref[...]
        if mode == "log_sigmoid":
            z = _log_sigmoid(z)
    if n_prefix:
        pre_ref, o_ref = refs
        o_ref[0, :n_prefix, :] = pre_ref[...].astype(o_ref.dtype)
        o_ref[0, n_prefix:, :] = z.astype(o_ref.dtype)
    else:
        (o_ref,) = refs
        o_ref[0] = z.astype(o_ref.dtype)


def _proj(x, w, bias, *, mode, out_dtype, prefix=None, tn=512):
    bsz, rows, d = x.shape
    n = w.shape[1] // 2 if mode == "glu" else w.shape[1]
    tn = min(tn, n)
    n_t = n // tn
    n_prefix = 0 if prefix is None else prefix.shape[0]
    in_specs = [pl.BlockSpec((1, rows, d), lambda b, j: (b, 0, 0))]
    args = [x]
    if mode == "glu":
        in_specs += [
            pl.BlockSpec((d, tn), lambda b, j: (0, j)),
            pl.BlockSpec((d, tn), lambda b, j: (0, j + n_t)),
            pl.BlockSpec((1, tn), lambda b, j: (0, j)),
            pl.BlockSpec((1, tn), lambda b, j: (0, j + n_t)),
        ]
        args += [w, w, bias, bias]
    else:
        in_specs += [pl.BlockSpec((d, tn), lambda b, j: (0, j)), pl.BlockSpec((1, tn), lambda b, j: (0, j))]
        args += [w, bias]
    if n_prefix:
        in_specs.append(pl.BlockSpec((n_prefix, tn), lambda b, j: (0, j)))
        args.append(prefix)
    return pl.pallas_call(
        functools.partial(_proj_kernel, mode=mode, n_prefix=n_prefix),
        grid=(bsz, n_t),
        in_specs=in_specs,
        out_specs=pl.BlockSpec((1, n_prefix + rows, tn), lambda b, j: (b, 0, j)),
        out_shape=jax.ShapeDtypeStruct((bsz, n_prefix + rows, n), out_dtype),
        compiler_params=_cparams(("parallel", "arbitrary")),
        name="in_proj_" + mode,
    )(*args)


def _cumsum_kernel(x_ref, o_ref, *, chunk):
    length, lanes = x_ref.shape
    r = lax.broadcasted_iota(jnp.int32, (chunk, chunk), 0)
    c = lax.broadcasted_iota(jnp.int32, (chunk, chunk), 1)
    tri = jnp.where(c <= r, 1.0, 0.0).astype(BF16)
    carry = jnp.zeros((1, lanes), F32)
    for start in range(0, length, chunk):
        size = min(chunk, length - start)
        local = _dot01(tri[:size, :size], x_ref[start:start + size, :])
        o_ref[start:start + size, :] = local + carry
        carry = carry + local[size - 1:size, :]


def _cumsum_rows(x, chunk=256):
    length, lanes = x.shape
    tl = V7X_LANES if lanes % V7X_LANES == 0 else lanes
    return pl.pallas_call(
        functools.partial(_cumsum_kernel, chunk=chunk),
        grid=(lanes // tl,),
        in_specs=[pl.BlockSpec((length, tl), lambda i: (0, i))],
        out_specs=pl.BlockSpec((length, tl), lambda i: (0, i)),
        out_shape=jax.ShapeDtypeStruct(x.shape, F32),
        compiler_params=_cparams(("parallel",)),
        name="logf_cumsum",
    )(x)


def _attn_prompt_kernel(q_ref, k_ref, v_ref, dkm_ref, dkr_ref, o_ref, *, tq, scale):
    seq = q_ref.shape[1]
    kb = k_ref[0].astype(BF16)
    vb = v_ref[0].astype(BF16)
    k_meta, v_meta = kb[:N_META], vb[:N_META]
    bias_meta = -dkm_ref[0, 0]
    contract_last = (((1,), (1,)), ((), ()))
    for qi in range(seq // tq):
        n_keys = (qi + 1) * tq
        q = q_ref[0, qi * tq:(qi + 1) * tq, :]
        k_real = kb[N_META:N_META + n_keys]
        v_real = vb[N_META:N_META + n_keys]
        s = lax.dot_general(q, k_real, contract_last, preferred_element_type=F32) * scale
        s = s - dkr_ref[0, 0, :, :n_keys]
        row = qi * tq + lax.broadcasted_iota(jnp.int32, (tq, n_keys), 0)
        col = lax.broadcasted_iota(jnp.int32, (tq, n_keys), 1)
        s = jnp.where(col <= row, s, NEG_INF)
        s_meta = lax.dot_general(q, k_meta, contract_last, preferred_element_type=F32) * scale + bias_meta
        m = jnp.maximum(jnp.max(s, axis=-1, keepdims=True), jnp.max(s_meta, axis=-1, keepdims=True))
        p = jnp.exp(s - m)
        p_meta = jnp.exp(s_meta - m)
        denom = jnp.sum(p, axis=-1, keepdims=True) + jnp.sum(p_meta, axis=-1, keepdims=True)
        o = jnp.dot(p.astype(BF16), v_real, preferred_element_type=F32)
        o = o + jnp.dot(p_meta.astype(BF16), v_meta, preferred_element_type=F32)
        o_ref[0, qi * tq:(qi + 1) * tq, :] = (o / denom).astype(o_ref.dtype)


def _attn_prompt(q, k, v, dk_meta, dk_real, *, tq_pref=256):
    bsz, seq, _ = q.shape
    length = k.shape[1]
    tq = _row_tile(seq, tq_pref)
    return pl.pallas_call(
        functools.partial(_attn_prompt_kernel, tq=tq, scale=HEAD_DIM ** -0.5),
        grid=(bsz, N_HEADS),
        in_specs=[
            pl.BlockSpec((1, seq, HEAD_DIM), lambda b, h: (b, 0, h)),
            pl.BlockSpec((1, length, HEAD_DIM), lambda b, h: (b, 0, h)),
            pl.BlockSpec((1, length, HEAD_DIM), lambda b, h: (b, 0, h)),
            pl.BlockSpec((1, 1, 1, N_META), lambda b, h: (b, h, 0, 0)),
            pl.BlockSpec((1, 1, 1, seq), lambda b, h: (b, h, 0, 0)),
        ],
        out_specs=pl.BlockSpec((1, seq, HEAD_DIM), lambda b, h: (b, 0, h)),
        out_shape=jax.ShapeDtypeStruct(q.shape, BF16),
        compiler_params=_cparams(("parallel", "arbitrary")),
        name="attn_prompt",
    )(q, k, v, dk_meta, dk_real)


def _conv_kernel(u_ref, cw_ref, cb_ref, g_ref, b_ref, o_ref, h_ref, *, tr, off, sub_rows):
    channels = o_ref.shape[2]
    base = pl.multiple_of(pl.program_id(1) * tr, V7X_SUBLANES) if tr % V7X_SUBLANES == 0 else 0
    window = u_ref.at[0, pl.ds(base, tr + off + CONV_WIDTH - 1), :]
    for c0 in range(0, channels, V7X_LANES):
        for r0 in range(0, tr, sub_rows):
            acc = jnp.zeros((sub_rows, V7X_LANES), F32)
            for w in range(CONV_WIDTH):
                rows = window[r0 + off + w:r0 + off + w + sub_rows, c0:c0 + V7X_LANES]
                acc = acc + cw_ref[w:w + 1, c0:c0 + V7X_LANES] * rows
            h_ref[r0:r0 + sub_rows, c0:c0 + V7X_LANES] = acc
    hn = _layer_norm(h_ref[...] + cb_ref[...], g_ref[...], b_ref[...])
    o_ref[0] = (hn * jax.nn.sigmoid(hn)).astype(o_ref.dtype)


def _conv_branch(hist, conv_w, conv_b, ln_g, ln_b, *, n_out, off, tr_pref=256):
    bsz, hist_rows, channels = hist.shape
    tr = _row_tile(n_out, tr_pref) if n_out % 16 == 0 else n_out
    assert hist_rows >= n_out + off + CONV_WIDTH - 1
    sub_rows = math.gcd(tr, 64)
    return pl.pallas_call(
        functools.partial(_conv_kernel, tr=tr, off=off, sub_rows=sub_rows),
        grid=(bsz, n_out // tr),
        in_specs=[
            pl.BlockSpec((1, hist_rows, channels), lambda b, r: (b, 0, 0)),
            pl.BlockSpec((CONV_WIDTH, channels), lambda b, r: (0, 0)),
            pl.BlockSpec((1, channels), lambda b, r: (0, 0)),
            pl.BlockSpec((1, channels), lambda b, r: (0, 0)),
            pl.BlockSpec((1, channels), lambda b, r: (0, 0)),
        ],
        out_specs=pl.BlockSpec((1, tr, channels), lambda b, r: (b, r, 0)),
        out_shape=jax.ShapeDtypeStruct((bsz, n_out, channels), BF16),
        scratch_shapes=[pltpu.VMEM((tr, channels), F32)],
        compiler_params=_cparams(("parallel", "arbitrary")),
        name="conv_branch",
    )(hist, conv_w, conv_b, ln_g, ln_b)


def _merge_kernel(x_ref, hc_ref, att_ref, wgc_ref, wga_ref, wco_ref, wap_ref, bgc_ref, bga_ref, o_ref):
    x = x_ref[...]
    g_conv = jax.nn.sigmoid(jnp.dot(x, wgc_ref[...], preferred_element_type=F32) + bgc_ref[...])
    g_att = jax.nn.sigmoid(jnp.dot(x, wga_ref[...], preferred_element_type=F32) + bga_ref[...])
    conv_d = jnp.dot(hc_ref[...], wco_ref[...], preferred_element_type=F32)
    att_d = jnp.dot(att_ref[...], wap_ref[...], preferred_element_type=F32)
    o_ref[...] = (g_conv * conv_d + g_att * att_d).astype(o_ref.dtype)


def _merge(xb, h_conv, att, w_gc, w_ga, w_co, w_ap, b_gc, b_ga, *, tm_pref=1024, tn=512):
    rows, d = xb.shape
    d_conv = h_conv.shape[1]
    d_att = att.shape[1]
    tm = _row_tile(rows, tm_pref)
    return pl.pallas_call(
        _merge_kernel,
        grid=(rows // tm, d // tn),
        in_specs=[
            pl.BlockSpec((tm, d), lambda i, j: (i, 0)),
            pl.BlockSpec((tm, d_conv), lambda i, j: (i, 0)),
            pl.BlockSpec((tm, d_att), lambda i, j: (i, 0)),
            pl.BlockSpec((d, tn), lambda i, j: (0, j)),
            pl.BlockSpec((d, tn), lambda i, j: (0, j)),
            pl.BlockSpec((d_conv, tn), lambda i, j: (0, j)),
            pl.BlockSpec((d_att, tn), lambda i, j: (0, j)),
            pl.BlockSpec((1, tn), lambda i, j: (0, j)),
            pl.BlockSpec((1, tn), lambda i, j: (0, j)),
        ],
        out_specs=pl.BlockSpec((tm, tn), lambda i, j: (i, j)),
        out_shape=jax.ShapeDtypeStruct((rows, d), BF16),
        compiler_params=_cparams(("parallel", "arbitrary")),
        name="branch_merge",
    )(xb, h_conv, att, w_gc, w_ga, w_co, w_ap, b_gc, b_ga)


def _mix_out_kernel(x_ref, m_ref, wo_ref, g_ref, b_ref, o_ref, *, alpha):
    y = alpha * x_ref[...] + jnp.dot(m_ref[...], wo_ref[...], preferred_element_type=F32)
    o_ref[...] = _layer_norm(y, g_ref[...], b_ref[...])


def _mix_out(x, m, w_o, g, b, *, alpha, tm_pref=512):
    rows, d = x.shape
    tm = _row_tile(rows, tm_pref)
    return pl.pallas_call(
        functools.partial(_mix_out_kernel, alpha=alpha),
        grid=(rows // tm,),
        in_specs=[
            pl.BlockSpec((tm, d), lambda i: (i, 0)),
            pl.BlockSpec((tm, d), lambda i: (i, 0)),
            pl.BlockSpec((d, d), lambda i: (0, 0)),
            pl.BlockSpec((1, d), lambda i: (0, 0)),
            pl.BlockSpec((1, d), lambda i: (0, 0)),
        ],
        out_specs=pl.BlockSpec((tm, d), lambda i: (i, 0)),
        out_shape=jax.ShapeDtypeStruct((rows, d), F32),
        compiler_params=_cparams(("parallel",)),
        name="mixer_out",
    )(x, m, w_o, g, b)


def _gather_pages_kernel(pt_ref, *refs, pages_per_step):
    del pt_ref
    o_ref = refs[pages_per_step]
    for i in range(pages_per_step):
        o_ref[0, i] = refs[i][0]


def _gather_logf_pages(cache_logf, page_table, *, pages_per_step=16):
    bd, n_pages = page_table.shape
    _, page, heads = cache_logf.shape
    pp = math.gcd(n_pages, pages_per_step)

    def page_map(i):
        return lambda b, j, pt: (pt[b, j * pp + i], 0, 0)

    return pl.pallas_call(
        functools.partial(_gather_pages_kernel, pages_per_step=pp),
        grid_spec=pltpu.PrefetchScalarGridSpec(
            num_scalar_prefetch=1,
            grid=(bd, n_pages // pp),
            in_specs=[pl.BlockSpec((1, page, heads), page_map(i)) for i in range(pp)],
            out_specs=pl.BlockSpec((1, pp, page, heads), lambda b, j, pt: (b, j, 0, 0)),
        ),
        out_shape=jax.ShapeDtypeStruct((bd, n_pages, page, heads), F32),
        compiler_params=_cparams(("parallel", "arbitrary")),
        name="gather_logf_pages",
    )(page_table, *([cache_logf] * pp))


def _attn_sample_kernel(pt_ref, q_ref, kn_ref, vn_ref, gn_ref, g_ref, *refs, pages_per_step, n_tok, scale):
    del pt_ref
    pp = pages_per_step
    k_refs, v_refs = refs[:pp], refs[pp:2 * pp]
    o_ref, m_ref, l_ref, acc_ref = refs[2 * pp:]
    j = pl.program_id(1)
    n_rows = n_tok * N_HEADS
    page_rows = PAGE_SIZE * N_HEADS
    q = q_ref[0]
    contract_last = (((1,), (1,)), ((), ()))

    def same_head(n_cols):
        row_head = lax.broadcasted_iota(jnp.int32, (n_rows, n_cols), 0) % N_HEADS
        col_head = lax.broadcasted_iota(jnp.int32, (n_rows, n_cols), 1) % N_HEADS
        return row_head == col_head

    def online_update(scores, values):
        m_old = m_ref[...]
        m_new = m_old
        for s in scores:
            m_new = jnp.maximum(m_new, jnp.max(s, axis=-1, keepdims=True))
        corr = jnp.exp(m_old - m_new)
        l_new = l_ref[...] * corr
        acc = acc_ref[...] * corr
        for s, v in zip(scores, values):
            p = jnp.exp(s - m_new)
            l_new = l_new + jnp.sum(p, axis=-1, keepdims=True)
            acc = acc + jnp.dot(p.astype(BF16), v, preferred_element_type=F32)
        l_ref[...] = l_new
        acc_ref[...] = acc
        m_ref[...] = m_new

    @pl.when(j == 0)
    def _new_tokens():
        n_cols = NEW_TOKENS * N_HEADS
        s = lax.dot_general(q, kn_ref[0], contract_last, preferred_element_type=F32) * scale - gn_ref[0]
        key = lax.broadcasted_iota(jnp.int32, (n_rows, n_cols), 1) // N_HEADS
        tok = lax.broadcasted_iota(jnp.int32, (n_rows, n_cols), 0) // N_HEADS
        s = jnp.where(same_head(n_cols) & (key <= tok), s, NEG_INF)
        m_ref[...] = jnp.full_like(m_ref, -jnp.inf)
        l_ref[...] = jnp.zeros_like(l_ref)
        acc_ref[...] = jnp.zeros_like(acc_ref)
        online_update([s], [vn_ref[0]])

    page_mask = same_head(page_rows)
    scores, values = [], []
    for i in range(pp):
        k2 = k_refs[i][0, 0].reshape(page_rows, HEAD_DIM).astype(BF16)
        s = lax.dot_general(q, k2, contract_last, preferred_element_type=F32) * scale - g_ref[0, 0, i:i + 1, :]
        scores.append(jnp.where(page_mask, s, NEG_INF))
        values.append(v_refs[i][0, 0].reshape(page_rows, HEAD_DIM).astype(BF16))
    online_update(scores, values)

    @pl.when(j == pl.num_programs(1) - 1)
    def _finish():
        o_ref[0] = (acc_ref[...] / l_ref[...]).astype(o_ref.dtype)


def _attn_sample(q, k_new, v_new, g_new, g_past, cache_k, cache_v, layer, page_table, *, pages_per_step=8):
    bd, n_tok, d_att = q.shape
    n_pages = page_table.shape[1]
    pp = math.gcd(n_pages, pages_per_step)
    n_rows = n_tok * N_HEADS
    new_rows = NEW_TOKENS * N_HEADS
    page_rows = PAGE_SIZE * N_HEADS
    assert n_tok <= NEW_TOKENS
    pad_t = ((0, 0), (0, NEW_TOKENS - n_tok), (0, 0))
    head_rows = lambda a: jnp.pad(a, pad_t).reshape(bd, new_rows, HEAD_DIM).astype(BF16)
    q2 = q.reshape(bd, n_rows, HEAD_DIM)
    gn = jnp.pad(g_new, pad_t).reshape(bd, 1, new_rows)
    gp = g_past.reshape(bd, n_pages // pp, pp, page_rows)

    def page_map(i):
        return lambda b, j, pt: (layer, pt[b, j * pp + i], 0, 0, 0)

    kv_specs = [pl.BlockSpec((1, 1, PAGE_SIZE, N_HEADS, HEAD_DIM), page_map(i)) for i in range(pp)]
    per_batch = lambda b, j, pt: (b, 0, 0)
    grid_spec = pltpu.PrefetchScalarGridSpec(
        num_scalar_prefetch=1,
        grid=(bd, n_pages // pp),
        in_specs=[
            pl.BlockSpec((1, n_rows, HEAD_DIM), per_batch),
            pl.BlockSpec((1, new_rows, HEAD_DIM), per_batch),
            pl.BlockSpec((1, new_rows, HEAD_DIM), per_batch),
            pl.BlockSpec((1, 1, new_rows), per_batch),
            pl.BlockSpec((1, 1, pp, page_rows), lambda b, j, pt: (b, j, 0, 0)),
        ] + kv_specs + kv_specs,
        out_specs=pl.BlockSpec((1, n_rows, HEAD_DIM), per_batch),
        scratch_shapes=[
            pltpu.VMEM((n_rows, 1), F32),
            pltpu.VMEM((n_rows, 1), F32),
            pltpu.VMEM((n_rows, HEAD_DIM), F32),
        ],
    )
    out = pl.pallas_call(
        functools.partial(_attn_sample_kernel, pages_per_step=pp, n_tok=n_tok, scale=HEAD_DIM ** -0.5),
        grid_spec=grid_spec,
        out_shape=jax.ShapeDtypeStruct((bd, n_rows, HEAD_DIM), BF16),
        compiler_params=_cparams(("parallel", "arbitrary")),
        name="attn_sample",
    )(page_table, q2, head_rows(k_new), head_rows(v_new), gn, gp, *([cache_k] * pp), *([cache_v] * pp))
    return out.reshape(bd, n_tok, d_att)


def kernel(x_prompt, x_sample, cache_k, cache_v, cache_logf, state_conv, page_table, meta_tokens,
           w_ffn1_in, w_ffn1_out, ln1_g, ln1_b, w_in, b_in, conv_w, conv_b, conv_ln_g, conv_ln_b,
           w_conv_out, w_att_proj, w_o, ln2_g, ln2_b, w_ffn2_in, w_ffn2_out, ln3_g, ln3_b):
    depth = w_ffn1_in.shape[0]
    assert depth == 1, "the meta rows are carried through a single layer only"
    bsz, seq, d = x_prompt.shape
    bd, n_tok, _ = x_sample.shape
    d_att = N_HEADS * HEAD_DIM
    d_conv = conv_w.shape[2]
    alpha = (2.0 * depth) ** 0.25
    o_q = 2 * d_conv
    o_k, o_v, o_f = o_q + d_att, o_q + 2 * d_att, o_q + 3 * d_att
    o_gc = o_f + FORGET_SPLIT
    o_ga = o_gc + d
    row = lambda a: a.reshape(1, -1)

    xp = x_prompt.reshape(bsz * seq, d)
    xs = jnp.concatenate([meta_tokens.astype(F32), x_sample.reshape(bd * n_tok, d)], axis=0)
    outs = [[] for _ in range(8)]
    for l in range(depth):
        w1i, w1o = w_ffn1_in[l].astype(BF16), w_ffn1_out[l].astype(BF16)
        w2i, w2o = w_ffn2_in[l].astype(BF16), w_ffn2_out[l].astype(BF16)
        wl, bl = w_in[l], b_in[l]
        w_u, b_u = wl[:, :o_q].astype(BF16), row(bl[:o_q])
        w_q, b_q = wl[:, o_q:o_k].astype(BF16), row(bl[o_q:o_k])
        w_k, b_k = wl[:, o_k:o_v].astype(BF16), row(bl[o_k:o_v])
        w_v, b_v = wl[:, o_v:o_f].astype(BF16), row(bl[o_v:o_f])
        w_f, b_f = wl[:, o_f:o_gc].astype(BF16), row(bl[o_f:o_gc])
        w_gc, b_gc = wl[:, o_gc:o_ga].astype(BF16), row(bl[o_gc:o_ga])
        w_ga, b_ga = wl[:, o_ga:].astype(BF16), row(bl[o_ga:])
        w_co, w_ap, w_oo = w_conv_out[l].astype(BF16), w_att_proj[l].astype(BF16), w_o[l].astype(BF16)
        g1, b1, g2, b2, g3, b3 = row(ln1_g[l]), row(ln1_b[l]), row(ln2_g[l]), row(ln2_b[l]), row(ln3_g[l]), row(ln3_b[l])
        cb, cg, cbb = row(conv_b[l]), row(conv_ln_g[l]), row(conv_ln_b[l])

        xp, xp_b = _ffn(xp, w1i, w1o, g1, b1, alpha=alpha, emit_bf16=True)
        xs, xs_b = _ffn(xs, w1i, w1o, g1, b1, alpha=alpha, emit_bf16=True)

        xs_b3 = xs_b[None]
        u_s = _proj(xs_b3, w_u, b_u, mode="glu", out_dtype=F32)[0]
        q_s = _proj(xs_b3, w_q, b_q, mode="linear", out_dtype=BF16)[0]
        k_s = _proj(xs_b3, w_k, b_k, mode="linear", out_dtype=F32)[0]
        v_s = _proj(xs_b3, w_v, b_v, mode="linear", out_dtype=F32)[0]
        f_s = _proj(xs_b3, w_f, b_f, mode="log_sigmoid", out_dtype=F32)[0]

        xp_b3 = xp_b.reshape(bsz, seq, d)
        u_pre = jnp.concatenate([jnp.zeros((N_META, d_conv), F32), u_s[:N_META]], axis=0)
        u_p = _proj(xp_b3, w_u, b_u, mode="glu", out_dtype=F32, prefix=u_pre)
        q_p = _proj(xp_b3, w_q, b_q, mode="linear", out_dtype=BF16)
        k_p = _proj(xp_b3, w_k, b_k, mode="linear", out_dtype=F32, prefix=k_s[:N_META])
        v_p = _proj(xp_b3, w_v, b_v, mode="linear", out_dtype=F32, prefix=v_s[:N_META])
        f_p = _proj(xp_b3, w_f, b_f, mode="log_sigmoid", out_dtype=F32, prefix=f_s[:N_META])

        length = N_META + seq
        f_lanes = f_p.transpose(1, 0, 2).reshape(length, bsz * N_HEADS)
        dk = _cumsum_rows(f_lanes).reshape(length, bsz, N_HEADS).transpose(1, 2, 0)
        dk_meta = dk[:, :, None, :N_META]
        dk_real = dk[:, :, None, N_META:]
        att_p = _attn_prompt(q_p, k_p, v_p, dk_meta, dk_real)
        hc_p = _conv_branch(u_p, conv_w[l], cb, cg, cbb, n_out=seq, off=2)
        m_p = _merge(xp_b, hc_p.reshape(bsz * seq, d_conv), att_p.reshape(bsz * seq, d_att),
                     w_gc, w_ga, w_co, w_ap, b_gc, b_ga)
        xp = _mix_out(xp, m_p, w_oo, g2, b2, alpha=alpha)

        sl = slice(N_META, None)
        u_new = u_s[sl].reshape(bd, n_tok, d_conv)
        hist = jnp.concatenate([state_conv[l].astype(F32), u_new], axis=1)
        hc_s = _conv_branch(hist, conv_w[l], cb, cg, cbb, n_out=n_tok, off=0)
        n_pages = page_table.shape[1]
        past = n_pages * PAGE_SIZE
        f_new = jnp.pad(f_s[sl].reshape(bd, n_tok, N_HEADS), ((0, 0), (0, CUMSUM_ROW_ALIGN - n_tok), (0, 0)))
        f_seq = jnp.concatenate([_gather_logf_pages(cache_logf[l], page_table).reshape(bd, past, N_HEADS), f_new], axis=1)
        total = past + CUMSUM_ROW_ALIGN
        g_seq = _cumsum_rows(f_seq.transpose(1, 0, 2).reshape(total, bd * N_HEADS))
        g_seq = g_seq.reshape(total, bd, N_HEADS).transpose(1, 0, 2)
        att_s = _attn_sample(
            q_s[sl].reshape(bd, n_tok, d_att), k_s[sl].reshape(bd, n_tok, d_att), v_s[sl].reshape(bd, n_tok, d_att),
            g_seq[:, past:past + n_tok], g_seq[:, :past].reshape(bd, n_pages, PAGE_SIZE, N_HEADS),
            cache_k, cache_v, l, page_table)
        xs_s = xs[sl]
        m_s = _merge(xs_b[sl], hc_s.reshape(bd * n_tok, d_conv), att_s.reshape(bd * n_tok, d_att),
                     w_gc, w_ga, w_co, w_ap, b_gc, b_ga)
        xs_s = _mix_out(xs_s, m_s, w_oo, g2, b2, alpha=alpha)

        xp = _ffn(xp, w2i, w2o, g3, b3, alpha=alpha, emit_bf16=False)
        xs_s = _ffn(xs_s, w2i, w2o, g3, b3, alpha=alpha, emit_bf16=False)
        xs = jnp.concatenate([meta_tokens.astype(F32), xs_s], axis=0)

        hd_shape = (N_HEADS, HEAD_DIM)
        outs[0].append(k_p.reshape(bsz, length, *hd_shape))
        outs[1].append(v_p.reshape(bsz, length, *hd_shape))
        outs[2].append(f_p)
        outs[3].append(u_p[:, -(CONV_WIDTH - 1):])
        outs[4].append(k_s[sl].reshape(bd, n_tok, *hd_shape))
        outs[5].append(v_s[sl].reshape(bd, n_tok, *hd_shape))
        outs[6].append(f_s[sl].reshape(bd, n_tok, N_HEADS))
        outs[7].append(hist[:, -(CONV_WIDTH - 1):])

    y_prompt = xp.reshape(bsz, seq, d)
    y_sample = xs[N_META:].reshape(bd, n_tok, d)
    return (y_prompt, y_sample) + tuple(jnp.stack(o, axis=0) for o in outs)
```

```python
import functools
import math

import jax
import jax.numpy as jnp
from jax import lax
from jax.experimental import pallas as pl
from jax.experimental.pallas import tpu as pltpu

F32 = jnp.float32
BF16 = jnp.bfloat16

N_META = 16
N_HEADS = 16
HEAD_DIM = 128
CONV_WIDTH = 31
PAGE_SIZE = 128
FORGET_SPLIT = N_HEADS
LN_EPS = 1e-5
NEG_INF = -1e30

V7X_VMEM_LIMIT_BYTES = 56 * 1024 * 1024
V7X_LANES = 128
V7X_SUBLANES = 8
NEW_TOKENS = 8
CUMSUM_ROW_ALIGN = 16
DECODE_PAGES_PER_STEP = 8
DECODE_SEQS_IN_ATTN = 8
DECODE_SEQS_IN_CONV = 4


def _cparams(semantics):
    return pltpu.CompilerParams(dimension_semantics=semantics, vmem_limit_bytes=V7X_VMEM_LIMIT_BYTES)


def _row_tile(rows, preferred):
    best = None
    for t in range(16, min(rows, preferred) + 1, 16):
        if rows % t == 0:
            best = t
    return best if best is not None else rows


def _layer_norm(y, g, b):
    mu = jnp.mean(y, axis=-1, keepdims=True)
    d = y - mu
    var = jnp.mean(d * d, axis=-1, keepdims=True)
    return d * lax.rsqrt(var + LN_EPS) * g + b


def _log_sigmoid(z):
    return jnp.minimum(z, 0.0) - jnp.log1p(jnp.exp(-jnp.abs(z)))


def _split3(x):
    hi = x.astype(BF16)
    r1 = x - hi.astype(F32)
    mid = r1.astype(BF16)
    lo = (r1 - mid.astype(F32)).astype(BF16)
    return hi, mid, lo


def _dot01(sel, x):
    hi, mid, lo = _split3(x)
    out = jnp.dot(sel, lo, preferred_element_type=F32)
    out = out + jnp.dot(sel, mid, preferred_element_type=F32)
    return out + jnp.dot(sel, hi, preferred_element_type=F32)


def _ffn_kernel(x_ref, wa_ref, wb_ref, wo_ref, g_ref, b_ref, *rest, alpha, n_f, emit_bf16):
    if emit_bf16:
        o_ref, ob_ref, xb_ref, acc_ref = rest
    else:
        o_ref, xb_ref, acc_ref = rest
    f = pl.program_id(1)

    @pl.when(f == 0)
    def _init():
        xb_ref[...] = x_ref[...].astype(BF16)
        acc_ref[...] = jnp.zeros_like(acc_ref)

    xb = xb_ref[...]
    a = jnp.dot(xb, wa_ref[...], preferred_element_type=F32)
    b = jnp.dot(xb, wb_ref[...], preferred_element_type=F32)
    h = (a * jax.nn.sigmoid(a) * b).astype(BF16)
    acc_ref[...] += jnp.dot(h, wo_ref[...], preferred_element_type=F32)

    @pl.when(f == n_f - 1)
    def _finish():
        y = alpha * x_ref[...] + 0.5 * acc_ref[...]
        o = _layer_norm(y, g_ref[...], b_ref[...])
        o_ref[...] = o
        if emit_bf16:
            ob_ref[...] = o.astype(BF16)


def _ffn(x, w_in, w_out, g, b, *, alpha, emit_bf16, tm_pref=512, tf=512):
    rows, d = x.shape
    d_ff = w_out.shape[0]
    tm = _row_tile(rows, tm_pref)
    n_f = d_ff // tf
    out_shape = [jax.ShapeDtypeStruct((rows, d), F32)]
    out_specs = [pl.BlockSpec((tm, d), lambda i, f: (i, 0))]
    if emit_bf16:
        out_shape.append(jax.ShapeDtypeStruct((rows, d), BF16))
        out_specs.append(pl.BlockSpec((tm, d), lambda i, f: (i, 0)))
    res = pl.pallas_call(
        functools.partial(_ffn_kernel, alpha=alpha, n_f=n_f, emit_bf16=emit_bf16),
        grid=(rows // tm, n_f),
        in_specs=[
            pl.BlockSpec((tm, d), lambda i, f: (i, 0)),
            pl.BlockSpec((d, tf), lambda i, f: (0, f)),
            pl.BlockSpec((d, tf), lambda i, f: (0, f + n_f)),
            pl.BlockSpec((tf, d), lambda i, f: (f, 0)),
            pl.BlockSpec((1, d), lambda i, f: (0, 0)),
            pl.BlockSpec((1, d), lambda i, f: (0, 0)),
        ],
        out_specs=out_specs,
        out_shape=out_shape,
        scratch_shapes=[pltpu.VMEM((tm, d), BF16), pltpu.VMEM((tm, d), F32)],
        compiler_params=_cparams(("parallel", "arbitrary")),
        name="ffn_sublayer",
    )(x, w_in, w_in, w_out, g, b)
    return res if emit_bf16 else res[0]


def _proj_kernel(x_ref, *refs, mode, n_prefix):
    x = x_ref[0]
    if mode == "glu":
        wa_ref, wg_ref, ba_ref, bg_ref = refs[:4]
        refs = refs[4:]
        za = jnp.dot(x, wa_ref[...], preferred_element_type=F32) + ba_ref[...]
        zg = jnp.dot(x, wg_ref[...], preferred_element_type=F32) + bg_ref[...]
        z = za * jax.nn.sigmoid(zg)
    else:
        w_ref, b_ref = refs[:2]
        refs = refs[2:]
        z = jnp.dot(x, w_ref[...], preferred_element_type=F32) + b_ref[...]
        if mode == "log_sigmoid":
            z = _log_sigmoid(z)
    if n_prefix:
        pre_ref, o_ref = refs
        o_ref[0, :n_prefix, :] = pre_ref[...].astype(o_ref.dtype)
        o_ref[0, n_prefix:, :] = z.astype(o_ref.dtype)
    else:
        (o_ref,) = refs
        o_ref[0] = z.astype(o_ref.dtype)


def _proj(x, w, bias, *, mode, out_dtype, prefix=None, tn=512):
    bsz, rows, d = x.shape
    n = w.shape[1] // 2 if mode == "glu" else w.shape[1]
    tn = min(tn, n)
    n_t = n // tn
    n_prefix = 0 if prefix is None else prefix.shape[0]
    in_specs = [pl.BlockSpec((1, rows, d), lambda b, j: (b, 0, 0))]
    args = [x]
    if mode == "glu":
        in_specs += [
            pl.BlockSpec((d, tn), lambda b, j: (0, j)),
            pl.BlockSpec((d, tn), lambda b, j: (0, j + n_t)),
            pl.BlockSpec((1, tn), lambda b, j: (0, j)),
            pl.BlockSpec((1, tn), lambda b, j: (0, j + n_t)),
        ]
        args += [w, w, bias, bias]
    else:
        in_specs += [pl.BlockSpec((d, tn), lambda b, j: (0, j)), pl.BlockSpec((1, tn), lambda b, j: (0, j))]
        args += [w, bias]
    if n_prefix:
        in_specs.append(pl.BlockSpec((n_prefix, tn), lambda b, j: (0, j)))
        args.append(prefix)
    return pl.pallas_call(
        functools.partial(_proj_kernel, mode=mode, n_prefix=n_prefix),
        grid=(bsz, n_t),
        in_specs=in_specs,
        out_specs=pl.BlockSpec((1, n_prefix + rows, tn), lambda b, j: (b, 0, j)),
        out_shape=jax.ShapeDtypeStruct((bsz, n_prefix + rows, n), out_dtype),
        compiler_params=_cparams(("parallel", "arbitrary")),
        name="in_proj_" + mode,
    )(*args)


def _cumsum_kernel(*refs, chunk):
    n_seg = len(refs) // 2
    lanes = refs[0].shape[1]
    r = lax.broadcasted_iota(jnp.int32, (chunk, chunk), 0)
    c = lax.broadcasted_iota(jnp.int32, (chunk, chunk), 1)
    tri = jnp.where(c <= r, 1.0, 0.0).astype(BF16)
    carry = jnp.zeros((1, lanes), F32)
    for x_ref, o_ref in zip(refs[:n_seg], refs[n_seg:]):
        length = x_ref.shape[0]
        for start in range(0, length, chunk):
            size = min(chunk, length - start)
            local = _dot01(tri[:size, :size], x_ref[start:start + size, :])
            o_ref[start:start + size, :] = local + carry
            carry = carry + local[size - 1:size, :]


def _cumsum_rows(*segments, chunk=256):
    lanes = segments[0].shape[1]
    tl = V7X_LANES if lanes % V7X_LANES == 0 else lanes
    specs = [pl.BlockSpec((s.shape[0], tl), lambda i: (0, i)) for s in segments]
    return pl.pallas_call(
        functools.partial(_cumsum_kernel, chunk=chunk),
        grid=(lanes // tl,),
        in_specs=specs,
        out_specs=specs,
        out_shape=[jax.ShapeDtypeStruct(s.shape, F32) for s in segments],
        compiler_params=_cparams(("parallel",)),
        name="logf_cumsum",
    )(*segments)


def _attn_prompt_kernel(q_ref, k_ref, v_ref, dkm_ref, dkr_ref, o_ref, *, tq, scale):
    seq = q_ref.shape[1]
    kb = k_ref[0].astype(BF16)
    vb = v_ref[0].astype(BF16)
    k_meta, v_meta = kb[:N_META], vb[:N_META]
    bias_meta = -dkm_ref[0, 0]
    contract_last = (((1,), (1,)), ((), ()))
    for qi in range(seq // tq):
        n_keys = (qi + 1) * tq
        q = q_ref[0, qi * tq:(qi + 1) * tq, :]
        k_real = kb[N_META:N_META + n_keys]
        v_real = vb[N_META:N_META + n_keys]
        s = lax.dot_general(q, k_real, contract_last, preferred_element_type=F32) * scale
        s = s - dkr_ref[0, 0, :, :n_keys]
        row = qi * tq + lax.broadcasted_iota(jnp.int32, (tq, n_keys), 0)
        col = lax.broadcasted_iota(jnp.int32, (tq, n_keys), 1)
        s = jnp.where(col <= row, s, NEG_INF)
        s_meta = lax.dot_general(q, k_meta, contract_last, preferred_element_type=F32) * scale + bias_meta
        m = jnp.maximum(jnp.max(s, axis=-1, keepdims=True), jnp.max(s_meta, axis=-1, keepdims=True))
        p = jnp.exp(s - m)
        p_meta = jnp.exp(s_meta - m)
        denom = jnp.sum(p, axis=-1, keepdims=True) + jnp.sum(p_meta, axis=-1, keepdims=True)
        o = jnp.dot(p.astype(BF16), v_real, preferred_element_type=F32)
        o = o + jnp.dot(p_meta.astype(BF16), v_meta, preferred_element_type=F32)
        o_ref[0, qi * tq:(qi + 1) * tq, :] = (o / denom).astype(o_ref.dtype)


def _attn_prompt(q, k, v, dk_meta, dk_real, *, tq_pref=256, **decode):
    bsz, seq, _ = q.shape
    length = k.shape[1]
    tq = _row_tile(seq, tq_pref)
    (out,), dec_out, n_seqs = _call_with_decode(
        functools.partial(_attn_prompt_kernel, tq=tq, scale=HEAD_DIM ** -0.5),
        grid=(bsz, N_HEADS),
        in_specs=[
            pl.BlockSpec((1, seq, HEAD_DIM), lambda b, h: (b, 0, h)),
            pl.BlockSpec((1, length, HEAD_DIM), lambda b, h: (b, 0, h)),
            pl.BlockSpec((1, length, HEAD_DIM), lambda b, h: (b, 0, h)),
            pl.BlockSpec((1, 1, 1, N_META), lambda b, h: (b, h, 0, 0)),
            pl.BlockSpec((1, 1, 1, seq), lambda b, h: (b, h, 0, 0)),
        ],
        out_specs=[pl.BlockSpec((1, seq, HEAD_DIM), lambda b, h: (b, 0, h))],
        out_shape=[jax.ShapeDtypeStruct(q.shape, BF16)],
        scratch_shapes=[],
        operands=(q, k, v, dk_meta, dk_real),
        name="attn_prompt",
        **decode,
    )
    return out, dec_out, n_seqs


def _conv_taps_by_shift(window, cw_ref, r0, c0, off, n_rows):
    sub = V7X_SUBLANES
    n_out = n_rows // sub
    n_win = -(-(n_rows + off + CONV_WIDTH - 1) // sub)
    win = window[r0:r0 + n_win * sub, c0:c0 + V7X_LANES].reshape(n_win, sub, V7X_LANES)
    sublane = lax.broadcasted_iota(jnp.int32, (1, sub, V7X_LANES), 1)
    out = jnp.zeros((n_out, sub, V7X_LANES), F32)
    for shift in range(sub):
        taps = [w for w in range(CONV_WIDTH) if (off + w) % sub == shift]
        if not taps:
            continue
        rotated = win if shift == 0 else pltpu.roll(win, sub - shift, axis=1)
        n_blocks = n_out if shift == 0 else n_out + 1
        part = jnp.zeros((n_blocks, sub, V7X_LANES), F32)
        for w in taps:
            a = (off + w) // sub
            part = part + cw_ref[w:w + 1, c0:c0 + V7X_LANES][None] * rotated[a:a + n_blocks]
        if shift == 0:
            out = out + part
        else:
            out = out + jnp.where(sublane < sub - shift, part[:n_out], part[1:])
    return out.reshape(n_rows, V7X_LANES)


def _conv_kernel(u_ref, cw_ref, cb_ref, g_ref, b_ref, o_ref, h_ref, *, tr, off, sub_rows):
    channels = o_ref.shape[2]
    base = pl.multiple_of(pl.program_id(1) * tr, V7X_SUBLANES) if tr % V7X_SUBLANES == 0 else 0
    window = u_ref.at[0, pl.ds(base, tr + off + CONV_WIDTH - 1), :]
    for c0 in range(0, channels, V7X_LANES):
        for r0 in range(0, tr, sub_rows):
            if sub_rows % V7X_SUBLANES == 0:
                acc = _conv_taps_by_shift(window, cw_ref, r0, c0, off, sub_rows)
            else:
                acc = jnp.zeros((sub_rows, V7X_LANES), F32)
                for w in range(CONV_WIDTH):
                    rows = window[r0 + off + w:r0 + off + w + sub_rows, c0:c0 + V7X_LANES]
                    acc = acc + cw_ref[w:w + 1, c0:c0 + V7X_LANES] * rows
            h_ref[r0:r0 + sub_rows, c0:c0 + V7X_LANES] = acc
    hn = _layer_norm(h_ref[...] + cb_ref[...], g_ref[...], b_ref[...])
    o_ref[0] = (hn * jax.nn.sigmoid(hn)).astype(o_ref.dtype)


def _conv_branch(hist, conv_w, conv_b, ln_g, ln_b, *, n_out, off, tr_pref=256, **decode):
    bsz, hist_rows, channels = hist.shape
    tr = _row_tile(n_out, tr_pref) if n_out % 16 == 0 else n_out
    assert hist_rows >= n_out + off + CONV_WIDTH - 1
    sub_rows = math.gcd(tr, 64)
    (out,), dec_out, n_seqs = _call_with_decode(
        functools.partial(_conv_kernel, tr=tr, off=off, sub_rows=sub_rows),
        grid=(bsz, n_out // tr),
        in_specs=[
            pl.BlockSpec((1, hist_rows, channels), lambda b, r: (b, 0, 0)),
            pl.BlockSpec((CONV_WIDTH, channels), lambda b, r: (0, 0)),
            pl.BlockSpec((1, channels), lambda b, r: (0, 0)),
            pl.BlockSpec((1, channels), lambda b, r: (0, 0)),
            pl.BlockSpec((1, channels), lambda b, r: (0, 0)),
        ],
        out_specs=[pl.BlockSpec((1, tr, channels), lambda b, r: (b, r, 0))],
        out_shape=[jax.ShapeDtypeStruct((bsz, n_out, channels), BF16)],
        scratch_shapes=[pltpu.VMEM((tr, channels), F32)],
        operands=(hist, conv_w, conv_b, ln_g, ln_b),
        name="conv_branch",
        **decode,
    )
    return out, dec_out, n_seqs


def _merge_kernel(x_ref, hc_ref, att_ref, wgc_ref, wga_ref, wco_ref, wap_ref, bgc_ref, bga_ref, o_ref):
    x = x_ref[...]
    g_conv = jax.nn.sigmoid(jnp.dot(x, wgc_ref[...], preferred_element_type=F32) + bgc_ref[...])
    g_att = jax.nn.sigmoid(jnp.dot(x, wga_ref[...], preferred_element_type=F32) + bga_ref[...])
    conv_d = jnp.dot(hc_ref[...], wco_ref[...], preferred_element_type=F32)
    att_d = jnp.dot(att_ref[...], wap_ref[...], preferred_element_type=F32)
    o_ref[...] = (g_conv * conv_d + g_att * att_d).astype(o_ref.dtype)


def _merge(xb, h_conv, att, w_gc, w_ga, w_co, w_ap, b_gc, b_ga, *, tm_pref=1024, tn=512):
    rows, d = xb.shape
    d_conv = h_conv.shape[1]
    d_att = att.shape[1]
    tm = _row_tile(rows, tm_pref)
    return pl.pallas_call(
        _merge_kernel,
        grid=(rows // tm, d // tn),
        in_specs=[
            pl.BlockSpec((tm, d), lambda i, j: (i, 0)),
            pl.BlockSpec((tm, d_conv), lambda i, j: (i, 0)),
            pl.BlockSpec((tm, d_att), lambda i, j: (i, 0)),
            pl.BlockSpec((d, tn), lambda i, j: (0, j)),
            pl.BlockSpec((d, tn), lambda i, j: (0, j)),
            pl.BlockSpec((d_conv, tn), lambda i, j: (0, j)),
            pl.BlockSpec((d_att, tn), lambda i, j: (0, j)),
            pl.BlockSpec((1, tn), lambda i, j: (0, j)),
            pl.BlockSpec((1, tn), lambda i, j: (0, j)),
        ],
        out_specs=pl.BlockSpec((tm, tn), lambda i, j: (i, j)),
        out_shape=jax.ShapeDtypeStruct((rows, d), BF16),
        compiler_params=_cparams(("parallel", "arbitrary")),
        name="branch_merge",
    )(xb, h_conv, att, w_gc, w_ga, w_co, w_ap, b_gc, b_ga)


def _mix_out_kernel(x_ref, m_ref, wo_ref, g_ref, b_ref, o_ref, *, alpha):
    y = alpha * x_ref[...] + jnp.dot(m_ref[...], wo_ref[...], preferred_element_type=F32)
    o_ref[...] = _layer_norm(y, g_ref[...], b_ref[...])


def _mix_out(x, m, w_o, g, b, *, alpha, tm_pref=512):
    rows, d = x.shape
    tm = _row_tile(rows, tm_pref)
    return pl.pallas_call(
        functools.partial(_mix_out_kernel, alpha=alpha),
        grid=(rows // tm,),
        in_specs=[
            pl.BlockSpec((tm, d), lambda i: (i, 0)),
            pl.BlockSpec((tm, d), lambda i: (i, 0)),
            pl.BlockSpec((d, d), lambda i: (0, 0)),
            pl.BlockSpec((1, d), lambda i: (0, 0)),
            pl.BlockSpec((1, d), lambda i: (0, 0)),
        ],
        out_specs=pl.BlockSpec((tm, d), lambda i: (i, 0)),
        out_shape=jax.ShapeDtypeStruct((rows, d), F32),
        compiler_params=_cparams(("parallel",)),
        name="mixer_out",
    )(x, m, w_o, g, b)


def _gather_pages_kernel(pt_ref, *refs, pages_per_step, seqs_per_step):
    del pt_ref
    o_ref = refs[-1]
    page = refs[0].shape[1]
    for p in range(pages_per_step):
        pieces = [refs[s * pages_per_step + p][0] for s in range(seqs_per_step)]
        o_ref[p * page:(p + 1) * page, :] = jnp.concatenate(pieces, axis=1)


def _gather_logf_pages(cache_logf, page_table, *, pages_per_step=8):
    bd, n_pages = page_table.shape
    _, page, heads = cache_logf.shape
    pp = math.gcd(n_pages, pages_per_step)
    sps = math.gcd(bd, V7X_LANES // heads)
    if sps * heads != V7X_LANES:
        sps = bd

    def page_map(s, p):
        return lambda g, j, pt: (pt[g * sps + s, j * pp + p], 0, 0)

    return pl.pallas_call(
        functools.partial(_gather_pages_kernel, pages_per_step=pp, seqs_per_step=sps),
        grid_spec=pltpu.PrefetchScalarGridSpec(
            num_scalar_prefetch=1,
            grid=(bd // sps, n_pages // pp),
            in_specs=[pl.BlockSpec((1, page, heads), page_map(s, p)) for s in range(sps) for p in range(pp)],
            out_specs=pl.BlockSpec((pp * page, sps * heads), lambda g, j, pt: (j, g)),
        ),
        out_shape=jax.ShapeDtypeStruct((n_pages * page, bd * heads), F32),
        compiler_params=_cparams(("parallel", "arbitrary")),
        name="gather_logf_pages",
    )(page_table, *([cache_logf] * (sps * pp)))


_CONTRACT_LAST = (((1,), (1,)), ((), ()))


def _same_head(n_rows, n_cols):
    row_head = lax.broadcasted_iota(jnp.int32, (n_rows, n_cols), 0) % N_HEADS
    col_head = lax.broadcasted_iota(jnp.int32, (n_rows, n_cols), 1) % N_HEADS
    return row_head == col_head


def _decode_update(state, scores, values):
    m_ref, l_ref, acc_ref = state
    m_old = m_ref[...]
    m_new = m_old
    for s in scores:
        m_new = jnp.maximum(m_new, jnp.max(s, axis=-1, keepdims=True))
    corr = jnp.exp(m_old - m_new)
    l_new = l_ref[...] * corr
    acc = acc_ref[...] * corr
    for s, v in zip(scores, values):
        p = jnp.exp(s - m_new)
        l_new = l_new + jnp.sum(p, axis=-1, keepdims=True)
        acc = acc + jnp.dot(p.astype(BF16), v, preferred_element_type=F32)
    l_ref[...] = l_new
    acc_ref[...] = acc
    m_ref[...] = m_new


def _decode_begin(q_ref, kn_ref, vn_ref, gn_ref, state, *, scale):
    m_ref, l_ref, acc_ref = state
    q = q_ref[0]
    n_rows = q.shape[0]
    n_cols = NEW_TOKENS * N_HEADS
    s = lax.dot_general(q, kn_ref[0], _CONTRACT_LAST, preferred_element_type=F32) * scale - gn_ref[0]
    key = lax.broadcasted_iota(jnp.int32, (n_rows, n_cols), 1) // N_HEADS
    tok = lax.broadcasted_iota(jnp.int32, (n_rows, n_cols), 0) // N_HEADS
    s = jnp.where(_same_head(n_rows, n_cols) & (key <= tok), s, NEG_INF)
    m_ref[...] = jnp.full_like(m_ref, -jnp.inf)
    l_ref[...] = jnp.zeros_like(l_ref)
    acc_ref[...] = jnp.zeros_like(acc_ref)
    _decode_update(state, [s], [vn_ref[0]])


def _decode_pages(q_ref, g_ref, k_refs, v_refs, state, *, scale):
    q = q_ref[0]
    page_rows = PAGE_SIZE * N_HEADS
    page_mask = _same_head(q.shape[0], page_rows)
    scores, values = [], []
    for i, (k_ref, v_ref) in enumerate(zip(k_refs, v_refs)):
        k2 = k_ref[0, 0].reshape(page_rows, HEAD_DIM).astype(BF16)
        s = lax.dot_general(q, k2, _CONTRACT_LAST, preferred_element_type=F32) * scale - g_ref[0, 0, i:i + 1, :]
        scores.append(jnp.where(page_mask, s, NEG_INF))
        values.append(v_ref[0, 0].reshape(page_rows, HEAD_DIM).astype(BF16))
    _decode_update(state, scores, values)


def _decode_end(o_ref, state):
    _, l_ref, acc_ref = state
    o_ref[0] = (acc_ref[...] / l_ref[...]).astype(o_ref.dtype)


def _decode_step(dec_in, o_ref, state, *, pages_per_step, n_groups, scale, lin, n_active, host_body=None):
    q_ref, kn_ref, vn_ref, gn_ref, g_ref = dec_in[:5]
    k_refs = dec_in[5:5 + pages_per_step]
    v_refs = dec_in[5 + pages_per_step:]
    group = lin % n_groups
    always = n_active is None
    active = None if always else lin < n_active

    def guard(cond):
        return cond if always else cond & active

    @pl.when(guard(group == 0))
    def _begin():
        _decode_begin(q_ref, kn_ref, vn_ref, gn_ref, state, scale=scale)

    if host_body is not None:
        host_body()
    if always:
        _decode_pages(q_ref, g_ref, k_refs, v_refs, state, scale=scale)
    else:
        pl.when(active)(lambda: _decode_pages(q_ref, g_ref, k_refs, v_refs, state, scale=scale))

    @pl.when(guard(group == n_groups - 1))
    def _end():
        _decode_end(o_ref, state)


class _DecodeStream:
    def __init__(self, q, k_new, v_new, g_new, g_past, cache_k, cache_v, layer, page_table, pages_per_step):
        bd, n_tok, d_att = q.shape
        n_pages = page_table.shape[1]
        assert n_tok <= NEW_TOKENS
        self.bd, self.n_tok, self.d_att = bd, n_tok, d_att
        self.pp = math.gcd(n_pages, pages_per_step)
        self.n_groups = n_pages // self.pp
        self.n_rows = n_tok * N_HEADS
        self.layer = layer
        self.page_table = page_table
        new_rows = NEW_TOKENS * N_HEADS
        page_rows = PAGE_SIZE * N_HEADS
        pad_t = ((0, 0), (0, NEW_TOKENS - n_tok), (0, 0))
        head_rows = lambda a: jnp.pad(a, pad_t).reshape(bd, new_rows, HEAD_DIM).astype(BF16)
        self.operands = [
            q.reshape(bd, self.n_rows, HEAD_DIM), head_rows(k_new), head_rows(v_new),
            jnp.pad(g_new, pad_t).reshape(bd, 1, new_rows),
            g_past.reshape(bd, self.n_groups, self.pp, page_rows),
        ] + [cache_k] * self.pp + [cache_v] * self.pp
        self.new_rows, self.page_rows = new_rows, page_rows

    def in_specs(self, locate):
        pp, layer = self.pp, self.layer

        def per_seq(*idx):
            return (locate(*idx[:-1])[0], 0, 0)

        def bias(*idx):
            seq, group = locate(*idx[:-1])
            return (seq, group, 0, 0)

        def page(i):
            def index(*idx):
                seq, group = locate(*idx[:-1])
                return (layer, idx[-1][seq, group * pp + i], 0, 0, 0)
            return index

        kv = [pl.BlockSpec((1, 1, PAGE_SIZE, N_HEADS, HEAD_DIM), page(i)) for i in range(pp)]
        return [
            pl.BlockSpec((1, self.n_rows, HEAD_DIM), per_seq),
            pl.BlockSpec((1, self.new_rows, HEAD_DIM), per_seq),
            pl.BlockSpec((1, self.new_rows, HEAD_DIM), per_seq),
            pl.BlockSpec((1, 1, self.new_rows), per_seq),
            pl.BlockSpec((1, 1, pp, self.page_rows), bias),
        ] + kv + kv

    def out_spec(self, locate, seq0):
        return pl.BlockSpec((1, self.n_rows, HEAD_DIM), lambda *idx: (locate(*idx[:-1])[0] - seq0, 0, 0))

    def out_shape(self, n_seq):
        return jax.ShapeDtypeStruct((n_seq, self.n_rows, HEAD_DIM), BF16)

    def scratch_shapes(self):
        return [pltpu.VMEM((self.n_rows, 1), F32), pltpu.VMEM((self.n_rows, 1), F32),
                pltpu.VMEM((self.n_rows, HEAD_DIM), F32)]

    def step_params(self):
        return dict(pages_per_step=self.pp, n_groups=self.n_groups, scale=HEAD_DIM ** -0.5)


def _hosted_kernel(pt_ref, *refs, host_kernel, n_host, n_dec_in, step_params, n_active):
    del pt_ref
    n_in, n_out, n_scratch = n_host
    host_in, refs = refs[:n_in], refs[n_in:]
    dec_in, refs = refs[:n_dec_in], refs[n_dec_in:]
    host_out, refs = refs[:n_out], refs[n_out:]
    dec_out, refs = refs[0], refs[1:]
    host_scratch, state = refs[:n_scratch], refs[n_scratch:]
    lin = pl.program_id(0) * pl.num_programs(1) + pl.program_id(1)
    _decode_step(dec_in, dec_out, state, lin=lin, n_active=n_active, **step_params,
                 host_body=lambda: host_kernel(*host_in, *host_out, *host_scratch))


def _call_with_decode(host_kernel, *, grid, in_specs, out_specs, out_shape, scratch_shapes, operands, name,
                      stream=None, seq0=0, max_seqs=0):
    params = _cparams(("arbitrary", "arbitrary"))
    steps = grid[0] * grid[1]
    n_seqs = 0 if stream is None else min(max_seqs, stream.bd - seq0, steps // stream.n_groups)
    if n_seqs <= 0:
        outs = pl.pallas_call(host_kernel, grid=grid, in_specs=in_specs, out_specs=out_specs, out_shape=out_shape,
                              scratch_shapes=scratch_shapes, compiler_params=params, name=name)(*operands)
        return outs, None, 0
    n_groups = stream.n_groups
    n_active = n_seqs * n_groups

    def locate(i, j):
        lin = jnp.minimum(i * grid[1] + j, n_active - 1)
        return seq0 + lin // n_groups, lin % n_groups

    def with_table(spec):
        return pl.BlockSpec(spec.block_shape, lambda *idx, f=spec.index_map: f(*idx[:-1]))

    dec_in_specs = stream.in_specs(locate)
    kernel_fn = functools.partial(
        _hosted_kernel, host_kernel=host_kernel, n_host=(len(in_specs), len(out_specs), len(scratch_shapes)),
        n_dec_in=len(dec_in_specs), step_params=stream.step_params(), n_active=None if n_active == steps else n_active)
    outs = pl.pallas_call(
        kernel_fn,
        grid_spec=pltpu.PrefetchScalarGridSpec(
            num_scalar_prefetch=1,
            grid=grid,
            in_specs=[with_table(s) for s in in_specs] + dec_in_specs,
            out_specs=[with_table(s) for s in out_specs] + [stream.out_spec(locate, seq0)],
            scratch_shapes=list(scratch_shapes) + stream.scratch_shapes(),
        ),
        out_shape=list(out_shape) + [stream.out_shape(n_seqs)],
        compiler_params=params,
        name=name + "_with_decode",
    )(stream.page_table, *operands, *stream.operands)
    return outs[:-1], outs[-1], n_seqs


def _attn_sample_kernel(pt_ref, *refs, n_dec_in, step_params):
    del pt_ref
    dec_in, o_ref, state = refs[:n_dec_in], refs[n_dec_in], refs[n_dec_in + 1:]
    _decode_step(dec_in, o_ref, state, lin=pl.program_id(0) * pl.num_programs(1) + pl.program_id(1),
                 n_active=None, **step_params)


def _attn_sample(stream, seq0):
    n_seqs = stream.bd - seq0
    locate = lambda b, j: (seq0 + b, j)
    dec_in_specs = stream.in_specs(locate)
    return pl.pallas_call(
        functools.partial(_attn_sample_kernel, n_dec_in=len(dec_in_specs), step_params=stream.step_params()),
        grid_spec=pltpu.PrefetchScalarGridSpec(
            num_scalar_prefetch=1,
            grid=(n_seqs, stream.n_groups),
            in_specs=dec_in_specs,
            out_specs=stream.out_spec(locate, seq0),
            scratch_shapes=stream.scratch_shapes(),
        ),
        out_shape=stream.out_shape(n_seqs),
        compiler_params=_cparams(("arbitrary", "arbitrary")),
        name="attn_sample",
    )(stream.page_table, *stream.operands)


def kernel(x_prompt, x_sample, cache_k, cache_v, cache_logf, state_conv, page_table, meta_tokens,
           w_ffn1_in, w_ffn1_out, ln1_g, ln1_b, w_in, b_in, conv_w, conv_b, conv_ln_g, conv_ln_b,
           w_conv_out, w_att_proj, w_o, ln2_g, ln2_b, w_ffn2_in, w_ffn2_out, ln3_g, ln3_b):
    depth = w_ffn1_in.shape[0]
    assert depth == 1, "the meta rows are carried through a single layer only"
    bsz, seq, d = x_prompt.shape
    bd, n_tok, _ = x_sample.shape
    d_att = N_HEADS * HEAD_DIM
    d_conv = conv_w.shape[2]
    alpha = (2.0 * depth) ** 0.25
    o_q = 2 * d_conv
    o_k, o_v, o_f = o_q + d_att, o_q + 2 * d_att, o_q + 3 * d_att
    o_gc = o_f + FORGET_SPLIT
    o_ga = o_gc + d
    row = lambda a: a.reshape(1, -1)

    xp = x_prompt.reshape(bsz * seq, d)
    xs = jnp.concatenate([meta_tokens.astype(F32), x_sample.reshape(bd * n_tok, d)], axis=0)
    outs = [[] for _ in range(8)]
    for l in range(depth):
        w1i, w1o = w_ffn1_in[l].astype(BF16), w_ffn1_out[l].astype(BF16)
        w2i, w2o = w_ffn2_in[l].astype(BF16), w_ffn2_out[l].astype(BF16)
        wl, bl = w_in[l], b_in[l]
        w_u, b_u = wl[:, :o_q].astype(BF16), row(bl[:o_q])
        w_q, b_q = wl[:, o_q:o_k].astype(BF16), row(bl[o_q:o_k])
        w_k, b_k = wl[:, o_k:o_v].astype(BF16), row(bl[o_k:o_v])
        w_v, b_v = wl[:, o_v:o_f].astype(BF16), row(bl[o_v:o_f])
        w_f, b_f = wl[:, o_f:o_gc].astype(BF16), row(bl[o_f:o_gc])
        w_gc, b_gc = wl[:, o_gc:o_ga].astype(BF16), row(bl[o_gc:o_ga])
        w_ga, b_ga = wl[:, o_ga:].astype(BF16), row(bl[o_ga:])
        w_co, w_ap, w_oo = w_conv_out[l].astype(BF16), w_att_proj[l].astype(BF16), w_o[l].astype(BF16)
        g1, b1, g2, b2, g3, b3 = row(ln1_g[l]), row(ln1_b[l]), row(ln2_g[l]), row(ln2_b[l]), row(ln3_g[l]), row(ln3_b[l])
        cb, cg, cbb = row(conv_b[l]), row(conv_ln_g[l]), row(conv_ln_b[l])

        xp, xp_b = _ffn(xp, w1i, w1o, g1, b1, alpha=alpha, emit_bf16=True)
        xs, xs_b = _ffn(xs, w1i, w1o, g1, b1, alpha=alpha, emit_bf16=True)

        xs_b3 = xs_b[None]
        u_s = _proj(xs_b3, w_u, b_u, mode="glu", out_dtype=F32)[0]
        q_s = _proj(xs_b3, w_q, b_q, mode="linear", out_dtype=BF16)[0]
        k_s = _proj(xs_b3, w_k, b_k, mode="linear", out_dtype=F32)[0]
        v_s = _proj(xs_b3, w_v, b_v, mode="linear", out_dtype=F32)[0]
        f_s = _proj(xs_b3, w_f, b_f, mode="log_sigmoid", out_dtype=F32)[0]

        xp_b3 = xp_b.reshape(bsz, seq, d)
        u_pre = jnp.concatenate([jnp.zeros((N_META, d_conv), F32), u_s[:N_META]], axis=0)
        u_p = _proj(xp_b3, w_u, b_u, mode="glu", out_dtype=F32, prefix=u_pre)
        q_p = _proj(xp_b3, w_q, b_q, mode="linear", out_dtype=BF16)
        k_p = _proj(xp_b3, w_k, b_k, mode="linear", out_dtype=F32, prefix=k_s[:N_META])
        v_p = _proj(xp_b3, w_v, b_v, mode="linear", out_dtype=F32, prefix=v_s[:N_META])
        f_p = _proj(xp_b3, w_f, b_f, mode="log_sigmoid", out_dtype=F32, prefix=f_s[:N_META])

        length = N_META + seq
        f_lanes = f_p.transpose(1, 0, 2).reshape(length, bsz * N_HEADS)
        dk = _cumsum_rows(f_lanes)[0].reshape(length, bsz, N_HEADS).transpose(1, 2, 0)
        dk_meta = dk[:, :, None, :N_META]
        dk_real = dk[:, :, None, N_META:]

        sl = slice(N_META, None)
        n_pages = page_table.shape[1]
        past = n_pages * PAGE_SIZE
        f_new = jnp.pad(f_s[sl].reshape(bd, n_tok, N_HEADS), ((0, 0), (0, CUMSUM_ROW_ALIGN - n_tok), (0, 0)))
        f_new = f_new.transpose(1, 0, 2).reshape(CUMSUM_ROW_ALIGN, bd * N_HEADS)
        g_past, g_new = _cumsum_rows(_gather_logf_pages(cache_logf[l], page_table), f_new)
        g_past = g_past.reshape(past, bd, N_HEADS).transpose(1, 0, 2)
        g_new = g_new.reshape(CUMSUM_ROW_ALIGN, bd, N_HEADS).transpose(1, 0, 2)[:, :n_tok]
        stream = _DecodeStream(
            q_s[sl].reshape(bd, n_tok, d_att), k_s[sl].reshape(bd, n_tok, d_att), v_s[sl].reshape(bd, n_tok, d_att),
            g_new, g_past.reshape(bd, n_pages, PAGE_SIZE, N_HEADS),
            cache_k, cache_v, l, page_table, DECODE_PAGES_PER_STEP)

        att_p, att_s0, n0 = _attn_prompt(q_p, k_p, v_p, dk_meta, dk_real,
                                         stream=stream, seq0=0, max_seqs=DECODE_SEQS_IN_ATTN)
        hc_p, att_s1, n1 = _conv_branch(u_p, conv_w[l], cb, cg, cbb, n_out=seq, off=2,
                                        stream=stream, seq0=n0, max_seqs=DECODE_SEQS_IN_CONV)
        att_parts = [a for a in (att_s0, att_s1) if a is not None]
        if n0 + n1 < bd:
            att_parts.append(_attn_sample(stream, n0 + n1))
        att_s = jnp.concatenate(att_parts, axis=0)
        m_p = _merge(xp_b, hc_p.reshape(bsz * seq, d_conv), att_p.reshape(bsz * seq, d_att),
                     w_gc, w_ga, w_co, w_ap, b_gc, b_ga)
        xp = _mix_out(xp, m_p, w_oo, g2, b2, alpha=alpha)

        u_new = u_s[sl].reshape(bd, n_tok, d_conv)
        hist = jnp.concatenate([state_conv[l].astype(F32), u_new], axis=1)
        hc_s, _, _ = _conv_branch(hist, conv_w[l], cb, cg, cbb, n_out=n_tok, off=0)
        xs_s = xs[sl]
        m_s = _merge(xs_b[sl], hc_s.reshape(bd * n_tok, d_conv), att_s.reshape(bd * n_tok, d_att),
                     w_gc, w_ga, w_co, w_ap, b_gc, b_ga)
        xs_s = _mix_out(xs_s, m_s, w_oo, g2, b2, alpha=alpha)

        xp = _ffn(xp, w2i, w2o, g3, b3, alpha=alpha, emit_bf16=False)
        xs_s = _ffn(xs_s, w2i, w2o, g3, b3, alpha=alpha, emit_bf16=False)
        xs = jnp.concatenate([meta_tokens.astype(F32), xs_s], axis=0)

        hd_shape = (N_HEADS, HEAD_DIM)
        outs[0].append(k_p.reshape(bsz, length, *hd_shape))
        outs[1].append(v_p.reshape(bsz, length, *hd_shape))
        outs[2].append(f_p)
        outs[3].append(u_p[:, -(CONV_WIDTH - 1):])
        outs[4].append(k_s[sl].reshape(bd, n_tok, *hd_shape))
        outs[5].append(v_s[sl].reshape(bd, n_tok, *hd_shape))
        outs[6].append(f_s[sl].reshape(bd, n_tok, N_HEADS))
        outs[7].append(hist[:, -(CONV_WIDTH - 1):])

    y_prompt = xp.reshape(bsz, seq, d)
    y_sample = xs[N_META:].reshape(bd, n_tok, d)
    return (y_prompt, y_sample) + tuple(jnp.stack(o, axis=0) for o in outs)
```

```python
import functools
import math

import jax
import jax.numpy as jnp
from jax import lax
from jax.experimental import pallas as pl
from jax.experimental.pallas import tpu as pltpu

F32 = jnp.float32
BF16 = jnp.bfloat16

N_META = 16
N_HEADS = 16
HEAD_DIM = 128
CONV_WIDTH = 31
PAGE_SIZE = 128
FORGET_SPLIT = N_HEADS
LN_EPS = 1e-5
NEG_INF = -1e30

V7X_VMEM_LIMIT_BYTES = 56 * 1024 * 1024
V7X_LANES = 128
V7X_SUBLANES = 8
NEW_TOKENS = 8
CUMSUM_ROW_ALIGN = 16
DECODE_PAGES_PER_STEP = 8
DECODE_SEQS_IN_ATTN = 16
DECODE_SEQS_IN_CONV = 16
CONV_ROW_TILE = 64


def _cparams(semantics):
    return pltpu.CompilerParams(dimension_semantics=semantics, vmem_limit_bytes=V7X_VMEM_LIMIT_BYTES)


def _row_tile(rows, preferred):
    best = None
    for t in range(16, min(rows, preferred) + 1, 16):
        if rows % t == 0:
            best = t
    return best if best is not None else rows


def _layer_norm(y, g, b):
    mu = jnp.mean(y, axis=-1, keepdims=True)
    d = y - mu
    var = jnp.mean(d * d, axis=-1, keepdims=True)
    return d * lax.rsqrt(var + LN_EPS) * g + b


def _log_sigmoid(z):
    return jnp.minimum(z, 0.0) - jnp.log1p(jnp.exp(-jnp.abs(z)))


def _split3(x):
    hi = x.astype(BF16)
    r1 = x - hi.astype(F32)
    mid = r1.astype(BF16)
    lo = (r1 - mid.astype(F32)).astype(BF16)
    return hi, mid, lo


def _dot01(sel, x):
    hi, mid, lo = _split3(x)
    out = jnp.dot(sel, lo, preferred_element_type=F32)
    out = out + jnp.dot(sel, mid, preferred_element_type=F32)
    return out + jnp.dot(sel, hi, preferred_element_type=F32)


def _ffn_kernel(x_ref, wa_ref, wb_ref, wo_ref, g_ref, b_ref, *rest, alpha, n_f, emit_bf16):
    if emit_bf16:
        o_ref, ob_ref, xb_ref, acc_ref = rest
    else:
        o_ref, xb_ref, acc_ref = rest
    f = pl.program_id(1)

    @pl.when(f == 0)
    def _init():
        xb_ref[...] = x_ref[...].astype(BF16)
        acc_ref[...] = jnp.zeros_like(acc_ref)

    xb = xb_ref[...]
    a = jnp.dot(xb, wa_ref[...], preferred_element_type=F32)
    b = jnp.dot(xb, wb_ref[...], preferred_element_type=F32)
    h = (a * jax.nn.sigmoid(a) * b).astype(BF16)
    acc_ref[...] += jnp.dot(h, wo_ref[...], preferred_element_type=F32)

    @pl.when(f == n_f - 1)
    def _finish():
        y = alpha * x_ref[...] + 0.5 * acc_ref[...]
        o = _layer_norm(y, g_ref[...], b_ref[...])
        o_ref[...] = o
        if emit_bf16:
            ob_ref[...] = o.astype(BF16)


def _ffn(x, w_in, w_out, g, b, *, alpha, emit_bf16, tm_pref=512, tf=512):
    rows, d = x.shape
    d_ff = w_out.shape[0]
    tm = _row_tile(rows, tm_pref)
    n_f = d_ff // tf
    out_shape = [jax.ShapeDtypeStruct((rows, d), F32)]
    out_specs = [pl.BlockSpec((tm, d), lambda i, f: (i, 0))]
    if emit_bf16:
        out_shape.append(jax.ShapeDtypeStruct((rows, d), BF16))
        out_specs.append(pl.BlockSpec((tm, d), lambda i, f: (i, 0)))
    res = pl.pallas_call(
        functools.partial(_ffn_kernel, alpha=alpha, n_f=n_f, emit_bf16=emit_bf16),
        grid=(rows // tm, n_f),
        in_specs=[
            pl.BlockSpec((tm, d), lambda i, f: (i, 0)),
            pl.BlockSpec((d, tf), lambda i, f: (0, f)),
            pl.BlockSpec((d, tf), lambda i, f: (0, f + n_f)),
            pl.BlockSpec((tf, d), lambda i, f: (f, 0)),
            pl.BlockSpec((1, d), lambda i, f: (0, 0)),
            pl.BlockSpec((1, d), lambda i, f: (0, 0)),
        ],
        out_specs=out_specs,
        out_shape=out_shape,
        scratch_shapes=[pltpu.VMEM((tm, d), BF16), pltpu.VMEM((tm, d), F32)],
        compiler_params=_cparams(("parallel", "arbitrary")),
        name="ffn_sublayer",
    )(x, w_in, w_in, w_out, g, b)
    return res if emit_bf16 else res[0]


def _proj_kernel(x_ref, *refs, mode, n_prefix):
    x = x_ref[0]
    if mode == "glu":
        wa_ref, wg_ref, ba_ref, bg_ref = refs[:4]
        refs = refs[4:]
        za = jnp.dot(x, wa_ref[...], preferred_element_type=F32) + ba_ref[...]
        zg = jnp.dot(x, wg_ref[...], preferred_element_type=F32) + bg_ref[...]
        z = za * jax.nn.sigmoid(zg)
    else:
        w_ref, b_ref = refs[:2]
        refs = refs[2:]
        z = jnp.dot(x, w_ref[...], preferred_element_type=F32) + b_ref[...]
        if mode == "log_sigmoid":
            z = _log_sigmoid(z)
        elif mode == "query":
            z = z * HEAD_DIM ** -0.5
    if n_prefix:
        pre_ref, o_ref = refs
        o_ref[0, :n_prefix, :] = pre_ref[...].astype(o_ref.dtype)
        o_ref[0, n_prefix:, :] = z.astype(o_ref.dtype)
    else:
        (o_ref,) = refs
        o_ref[0] = z.astype(o_ref.dtype)


def _proj(x, w, bias, *, mode, out_dtype, prefix=None, tn=512):
    bsz, rows, d = x.shape
    n = w.shape[1] // 2 if mode == "glu" else w.shape[1]
    tn = min(tn, n)
    n_t = n // tn
    n_prefix = 0 if prefix is None else prefix.shape[0]
    in_specs = [pl.BlockSpec((1, rows, d), lambda b, j: (b, 0, 0))]
    args = [x]
    if mode == "glu":
        in_specs += [
            pl.BlockSpec((d, tn), lambda b, j: (0, j)),
            pl.BlockSpec((d, tn), lambda b, j: (0, j + n_t)),
            pl.BlockSpec((1, tn), lambda b, j: (0, j)),
            pl.BlockSpec((1, tn), lambda b, j: (0, j + n_t)),
        ]
        args += [w, w, bias, bias]
    else:
        in_specs += [pl.BlockSpec((d, tn), lambda b, j: (0, j)), pl.BlockSpec((1, tn), lambda b, j: (0, j))]
        args += [w, bias]
    if n_prefix:
        in_specs.append(pl.BlockSpec((n_prefix, tn), lambda b, j: (0, j)))
        args.append(prefix)
    return pl.pallas_call(
        functools.partial(_proj_kernel, mode=mode, n_prefix=n_prefix),
        grid=(bsz, n_t),
        in_specs=in_specs,
        out_specs=pl.BlockSpec((1, n_prefix + rows, tn), lambda b, j: (b, 0, j)),
        out_shape=jax.ShapeDtypeStruct((bsz, n_prefix + rows, n), out_dtype),
        compiler_params=_cparams(("parallel", "arbitrary")),
        name="in_proj_" + mode,
    )(*args)


def _cumsum_kernel(*refs, chunk):
    n_seg = len(refs) // 2
    lanes = refs[0].shape[1]
    r = lax.broadcasted_iota(jnp.int32, (chunk, chunk), 0)
    c = lax.broadcasted_iota(jnp.int32, (chunk, chunk), 1)
    tri = jnp.where(c <= r, 1.0, 0.0).astype(BF16)
    carry = jnp.zeros((1, lanes), F32)
    for x_ref, o_ref in zip(refs[:n_seg], refs[n_seg:]):
        length = x_ref.shape[0]
        for start in range(0, length, chunk):
            size = min(chunk, length - start)
            local = _dot01(tri[:size, :size], x_ref[start:start + size, :])
            o_ref[start:start + size, :] = local + carry
            carry = carry + local[size - 1:size, :]


def _cumsum_rows(*segments, chunk=256):
    lanes = segments[0].shape[1]
    tl = V7X_LANES if lanes % V7X_LANES == 0 else lanes
    specs = [pl.BlockSpec((s.shape[0], tl), lambda i: (0, i)) for s in segments]
    return pl.pallas_call(
        functools.partial(_cumsum_kernel, chunk=chunk),
        grid=(lanes // tl,),
        in_specs=specs,
        out_specs=specs,
        out_shape=[jax.ShapeDtypeStruct(s.shape, F32) for s in segments],
        compiler_params=_cparams(("parallel",)),
        name="logf_cumsum",
    )(*segments)


def _attn_prompt_tiles(q_ref, k_ref, v_ref, dkm_ref, dkr_ref, o_ref, *, tq, tiles):
    n_max = N_META + (max(tiles) + 1) * tq
    kb = k_ref[0, :n_max, :].astype(BF16)
    vb = v_ref[0, :n_max, :].astype(BF16)
    k_meta, v_meta = kb[:N_META], vb[:N_META]
    bias_meta = -dkm_ref[0, 0]
    contract_last = (((1,), (1,)), ((), ()))
    for qi in tiles:
        n_keys = (qi + 1) * tq
        q = q_ref[0, qi * tq:(qi + 1) * tq, :]
        k_real = kb[N_META:N_META + n_keys]
        v_real = vb[N_META:N_META + n_keys]
        s = lax.dot_general(q, k_real, contract_last, preferred_element_type=F32)
        s = s - dkr_ref[0, 0, :, :n_keys]
        row = qi * tq + lax.broadcasted_iota(jnp.int32, (tq, n_keys), 0)
        col = lax.broadcasted_iota(jnp.int32, (tq, n_keys), 1)
        s = jnp.where(col <= row, s, NEG_INF)
        s_meta = lax.dot_general(q, k_meta, contract_last, preferred_element_type=F32) + bias_meta
        m = jnp.maximum(jnp.max(s, axis=-1, keepdims=True), jnp.max(s_meta, axis=-1, keepdims=True))
        p = jnp.exp(s - m)
        p_meta = jnp.exp(s_meta - m)
        denom = jnp.sum(p, axis=-1, keepdims=True) + jnp.sum(p_meta, axis=-1, keepdims=True)
        o = jnp.dot(p.astype(BF16), v_real, preferred_element_type=F32)
        o = o + jnp.dot(p_meta.astype(BF16), v_meta, preferred_element_type=F32)
        o_ref[0, qi * tq:(qi + 1) * tq, :] = (o / denom).astype(o_ref.dtype)


def _attn_prompt_kernel(*refs, tq, tile_groups):
    if len(tile_groups) == 1:
        _attn_prompt_tiles(*refs, tq=tq, tiles=tile_groups[0])
        return
    part = pl.program_id(1) % len(tile_groups)
    for i, tiles in enumerate(tile_groups):
        pl.when(part == i)(functools.partial(_attn_prompt_tiles, *refs, tq=tq, tiles=tiles))


def _attn_prompt(q, k, v, dk_meta, dk_real, *, tq_pref=512, **decode):
    bsz, seq, _ = q.shape
    length = k.shape[1]
    tq = _row_tile(seq, tq_pref)
    n_tiles = seq // tq
    if n_tiles % 4 == 0:
        tile_groups = [[], []]
        for i in range(n_tiles // 2):
            tile_groups[i % 2] += [i, n_tiles - 1 - i]
    elif n_tiles % 2 == 0:
        tile_groups = [list(range(n_tiles // 2)), list(range(n_tiles // 2, n_tiles))]
    else:
        tile_groups = [list(range(n_tiles))]
    assert sorted(t for g in tile_groups for t in g) == list(range(n_tiles))
    ns = len(tile_groups)
    (out,), dec_out, n_seqs = _call_with_decode(
        functools.partial(_attn_prompt_kernel, tq=tq, tile_groups=tile_groups),
        grid=(bsz, N_HEADS * ns),
        in_specs=[
            pl.BlockSpec((1, seq, HEAD_DIM), lambda b, g: (b, 0, g // ns)),
            pl.BlockSpec((1, length, HEAD_DIM), lambda b, g: (b, 0, g // ns)),
            pl.BlockSpec((1, length, HEAD_DIM), lambda b, g: (b, 0, g // ns)),
            pl.BlockSpec((1, 1, 1, N_META), lambda b, g: (b, g // ns, 0, 0)),
            pl.BlockSpec((1, 1, 1, seq), lambda b, g: (b, g // ns, 0, 0)),
        ],
        out_specs=[pl.BlockSpec((1, seq, HEAD_DIM), lambda b, g: (b, 0, g // ns))],
        out_shape=[jax.ShapeDtypeStruct(q.shape, BF16)],
        scratch_shapes=[],
        operands=(q, k, v, dk_meta, dk_real),
        name="attn_prompt",
        **decode,
    )
    return out, dec_out, n_seqs


def _conv_taps_by_shift(window, cw_ref, r0, c0, off, n_rows):
    sub = V7X_SUBLANES
    n_out = n_rows // sub
    n_win = -(-(n_rows + off + CONV_WIDTH - 1) // sub)
    win = window[r0:r0 + n_win * sub, c0:c0 + V7X_LANES].reshape(n_win, sub, V7X_LANES)
    sublane = lax.broadcasted_iota(jnp.int32, (1, sub, V7X_LANES), 1)
    out = jnp.zeros((n_out, sub, V7X_LANES), F32)
    for shift in range(sub):
        taps = [w for w in range(CONV_WIDTH) if (off + w) % sub == shift]
        if not taps:
            continue
        rotated = win if shift == 0 else pltpu.roll(win, sub - shift, axis=1)
        n_blocks = n_out if shift == 0 else n_out + 1
        part = jnp.zeros((n_blocks, sub, V7X_LANES), F32)
        for w in taps:
            a = (off + w) // sub
            part = part + cw_ref[w:w + 1, c0:c0 + V7X_LANES][None] * rotated[a:a + n_blocks]
        if shift == 0:
            out = out + part
        else:
            out = out + jnp.where(sublane < sub - shift, part[:n_out], part[1:])
    return out.reshape(n_rows, V7X_LANES)


def _conv_kernel(u_ref, cw_ref, cb_ref, g_ref, b_ref, o_ref, h_ref, *, tr, off, sub_rows):
    channels = o_ref.shape[2]
    base = pl.multiple_of(pl.program_id(1) * tr, V7X_SUBLANES) if tr % V7X_SUBLANES == 0 else 0
    window = u_ref.at[0, pl.ds(base, tr + off + CONV_WIDTH - 1), :]
    for c0 in range(0, channels, V7X_LANES):
        for r0 in range(0, tr, sub_rows):
            if sub_rows % V7X_SUBLANES == 0:
                acc = _conv_taps_by_shift(window, cw_ref, r0, c0, off, sub_rows)
            else:
                acc = jnp.zeros((sub_rows, V7X_LANES), F32)
                for w in range(CONV_WIDTH):
                    rows = window[r0 + off + w:r0 + off + w + sub_rows, c0:c0 + V7X_LANES]
                    acc = acc + cw_ref[w:w + 1, c0:c0 + V7X_LANES] * rows
            h_ref[r0:r0 + sub_rows, c0:c0 + V7X_LANES] = acc
    hn = _layer_norm(h_ref[...] + cb_ref[...], g_ref[...], b_ref[...])
    o_ref[0] = (hn * jax.nn.sigmoid(hn)).astype(o_ref.dtype)


def _conv_branch(hist, conv_w, conv_b, ln_g, ln_b, *, n_out, off, tr_pref=CONV_ROW_TILE, **decode):
    bsz, hist_rows, channels = hist.shape
    tr = _row_tile(n_out, tr_pref) if n_out % 16 == 0 else n_out
    assert hist_rows >= n_out + off + CONV_WIDTH - 1
    sub_rows = math.gcd(tr, 64)
    (out,), dec_out, n_seqs = _call_with_decode(
        functools.partial(_conv_kernel, tr=tr, off=off, sub_rows=sub_rows),
        grid=(bsz, n_out // tr),
        in_specs=[
            pl.BlockSpec((1, hist_rows, channels), lambda b, r: (b, 0, 0)),
            pl.BlockSpec((CONV_WIDTH, channels), lambda b, r: (0, 0)),
            pl.BlockSpec((1, channels), lambda b, r: (0, 0)),
            pl.BlockSpec((1, channels), lambda b, r: (0, 0)),
            pl.BlockSpec((1, channels), lambda b, r: (0, 0)),
        ],
        out_specs=[pl.BlockSpec((1, tr, channels), lambda b, r: (b, r, 0))],
        out_shape=[jax.ShapeDtypeStruct((bsz, n_out, channels), BF16)],
        scratch_shapes=[pltpu.VMEM((tr, channels), F32)],
        operands=(hist, conv_w, conv_b, ln_g, ln_b),
        name="conv_branch",
        **decode,
    )
    return out, dec_out, n_seqs


def _merge_kernel(x_ref, hc_ref, att_ref, wgc_ref, wga_ref, wco_ref, wap_ref, bgc_ref, bga_ref, o_ref):
    x = x_ref[...]
    g_conv = jax.nn.sigmoid(jnp.dot(x, wgc_ref[...], preferred_element_type=F32) + bgc_ref[...])
    g_att = jax.nn.sigmoid(jnp.dot(x, wga_ref[...], preferred_element_type=F32) + bga_ref[...])
    conv_d = jnp.dot(hc_ref[...], wco_ref[...], preferred_element_type=F32)
    att_d = jnp.dot(att_ref[...], wap_ref[...], preferred_element_type=F32)
    o_ref[...] = (g_conv * conv_d + g_att * att_d).astype(o_ref.dtype)


def _merge(xb, h_conv, att, w_gc, w_ga, w_co, w_ap, b_gc, b_ga, *, tm_pref=1024, tn=512):
    rows, d = xb.shape
    d_conv = h_conv.shape[1]
    d_att = att.shape[1]
    tm = _row_tile(rows, tm_pref)
    return pl.pallas_call(
        _merge_kernel,
        grid=(rows // tm, d // tn),
        in_specs=[
            pl.BlockSpec((tm, d), lambda i, j: (i, 0)),
            pl.BlockSpec((tm, d_conv), lambda i, j: (i, 0)),
            pl.BlockSpec((tm, d_att), lambda i, j: (i, 0)),
            pl.BlockSpec((d, tn), lambda i, j: (0, j)),
            pl.BlockSpec((d, tn), lambda i, j: (0, j)),
            pl.BlockSpec((d_conv, tn), lambda i, j: (0, j)),
            pl.BlockSpec((d_att, tn), lambda i, j: (0, j)),
            pl.BlockSpec((1, tn), lambda i, j: (0, j)),
            pl.BlockSpec((1, tn), lambda i, j: (0, j)),
        ],
        out_specs=pl.BlockSpec((tm, tn), lambda i, j: (i, j)),
        out_shape=jax.ShapeDtypeStruct((rows, d), BF16),
        compiler_params=_cparams(("parallel", "arbitrary")),
        name="branch_merge",
    )(xb, h_conv, att, w_gc, w_ga, w_co, w_ap, b_gc, b_ga)


def _mix_out_kernel(x_ref, m_ref, wo_ref, g_ref, b_ref, o_ref, *, alpha):
    y = alpha * x_ref[...] + jnp.dot(m_ref[...], wo_ref[...], preferred_element_type=F32)
    o_ref[...] = _layer_norm(y, g_ref[...], b_ref[...])


def _mix_out(x, m, w_o, g, b, *, alpha, tm_pref=512):
    rows, d = x.shape
    tm = _row_tile(rows, tm_pref)
    return pl.pallas_call(
        functools.partial(_mix_out_kernel, alpha=alpha),
        grid=(rows // tm,),
        in_specs=[
            pl.BlockSpec((tm, d), lambda i: (i, 0)),
            pl.BlockSpec((tm, d), lambda i: (i, 0)),
            pl.BlockSpec((d, d), lambda i: (0, 0)),
            pl.BlockSpec((1, d), lambda i: (0, 0)),
            pl.BlockSpec((1, d), lambda i: (0, 0)),
        ],
        out_specs=pl.BlockSpec((tm, d), lambda i: (i, 0)),
        out_shape=jax.ShapeDtypeStruct((rows, d), F32),
        compiler_params=_cparams(("parallel",)),
        name="mixer_out",
    )(x, m, w_o, g, b)


def _gather_pages_kernel(pt_ref, *refs, pages_per_step, seqs_per_step):
    del pt_ref
    o_ref = refs[-1]
    page = refs[0].shape[1]
    for p in range(pages_per_step):
        pieces = [refs[s * pages_per_step + p][0] for s in range(seqs_per_step)]
        o_ref[p * page:(p + 1) * page, :] = jnp.concatenate(pieces, axis=1)


def _gather_logf_pages(cache_logf, page_table, *, pages_per_step=8):
    bd, n_pages = page_table.shape
    _, page, heads = cache_logf.shape
    pp = math.gcd(n_pages, pages_per_step)
    sps = math.gcd(bd, V7X_LANES // heads)
    if sps * heads != V7X_LANES:
        sps = bd

    def page_map(s, p):
        return lambda g, j, pt: (pt[g * sps + s, j * pp + p], 0, 0)

    return pl.pallas_call(
        functools.partial(_gather_pages_kernel, pages_per_step=pp, seqs_per_step=sps),
        grid_spec=pltpu.PrefetchScalarGridSpec(
            num_scalar_prefetch=1,
            grid=(bd // sps, n_pages // pp),
            in_specs=[pl.BlockSpec((1, page, heads), page_map(s, p)) for s in range(sps) for p in range(pp)],
            out_specs=pl.BlockSpec((pp * page, sps * heads), lambda g, j, pt: (j, g)),
        ),
        out_shape=jax.ShapeDtypeStruct((n_pages * page, bd * heads), F32),
        compiler_params=_cparams(("parallel", "arbitrary")),
        name="gather_logf_pages",
    )(page_table, *([cache_logf] * (sps * pp)))


_CONTRACT_LAST = (((1,), (1,)), ((), ()))


def _same_head(n_rows, n_cols):
    row_head = lax.broadcasted_iota(jnp.int32, (n_rows, n_cols), 0) % N_HEADS
    col_head = lax.broadcasted_iota(jnp.int32, (n_rows, n_cols), 1) % N_HEADS
    return row_head == col_head


def _decode_update(state, scores, values):
    m_ref, l_ref, acc_ref = state
    m_old = m_ref[...]
    m_new = m_old
    for s in scores:
        m_new = jnp.maximum(m_new, jnp.max(s, axis=-1, keepdims=True))
    corr = jnp.exp(m_old - m_new)
    l_new = l_ref[...] * corr
    acc = acc_ref[...] * corr
    for s, v in zip(scores, values):
        p = jnp.exp(s - m_new)
        l_new = l_new + jnp.sum(p, axis=-1, keepdims=True)
        acc = acc + jnp.dot(p.astype(BF16), v, preferred_element_type=F32)
    l_ref[...] = l_new
    acc_ref[...] = acc
    m_ref[...] = m_new


def _decode_begin(q_ref, kn_ref, vn_ref, gn_ref, state):
    m_ref, l_ref, acc_ref = state
    q = q_ref[0]
    n_rows = q.shape[0]
    n_cols = NEW_TOKENS * N_HEADS
    s = lax.dot_general(q, kn_ref[0], _CONTRACT_LAST, preferred_element_type=F32) - gn_ref[0]
    key = lax.broadcasted_iota(jnp.int32, (n_rows, n_cols), 1) // N_HEADS
    tok = lax.broadcasted_iota(jnp.int32, (n_rows, n_cols), 0) // N_HEADS
    s = jnp.where(_same_head(n_rows, n_cols) & (key <= tok), s, NEG_INF)
    m_ref[...] = jnp.full_like(m_ref, -jnp.inf)
    l_ref[...] = jnp.zeros_like(l_ref)
    acc_ref[...] = jnp.zeros_like(acc_ref)
    _decode_update(state, [s], [vn_ref[0]])


def _decode_pages(q_ref, g_ref, k_refs, v_refs, state):
    q = q_ref[0]
    page_rows = PAGE_SIZE * N_HEADS
    page_mask = _same_head(q.shape[0], page_rows)
    scores, values = [], []
    for i, (k_ref, v_ref) in enumerate(zip(k_refs, v_refs)):
        k2 = k_ref[0, 0].reshape(page_rows, HEAD_DIM).astype(BF16)
        s = lax.dot_general(q, k2, _CONTRACT_LAST, preferred_element_type=F32) - g_ref[0, 0, i:i + 1, :]
        scores.append(jnp.where(page_mask, s, NEG_INF))
        values.append(v_ref[0, 0].reshape(page_rows, HEAD_DIM).astype(BF16))
    _decode_update(state, scores, values)


def _decode_end(o_ref, state):
    _, l_ref, acc_ref = state
    o_ref[0] = (acc_ref[...] / l_ref[...]).astype(o_ref.dtype)


def _decode_step(dec_in, o_ref, state, *, pages_per_step, n_groups, lin, n_active, host_body=None):
    q_ref, kn_ref, vn_ref, gn_ref, g_ref = dec_in[:5]
    k_refs = dec_in[5:5 + pages_per_step]
    v_refs = dec_in[5 + pages_per_step:]
    group = lin % n_groups
    always = n_active is None
    active = None if always else lin < n_active

    def guard(cond):
        return cond if always else cond & active

    @pl.when(guard(group == 0))
    def _begin():
        _decode_begin(q_ref, kn_ref, vn_ref, gn_ref, state)

    if host_body is not None:
        host_body()
    if always:
        _decode_pages(q_ref, g_ref, k_refs, v_refs, state)
    else:
        pl.when(active)(lambda: _decode_pages(q_ref, g_ref, k_refs, v_refs, state))

    @pl.when(guard(group == n_groups - 1))
    def _end():
        _decode_end(o_ref, state)


class _DecodeStream:
    def __init__(self, q, k_new, v_new, g_new, g_past, cache_k, cache_v, layer, page_table, pages_per_step):
        bd, n_tok, d_att = q.shape
        n_pages = page_table.shape[1]
        assert n_tok <= NEW_TOKENS
        self.bd, self.n_tok, self.d_att = bd, n_tok, d_att
        self.pp = math.gcd(n_pages, pages_per_step)
        self.n_groups = n_pages // self.pp
        self.n_rows = n_tok * N_HEADS
        self.layer = layer
        self.page_table = page_table
        new_rows = NEW_TOKENS * N_HEADS
        page_rows = PAGE_SIZE * N_HEADS
        pad_t = ((0, 0), (0, NEW_TOKENS - n_tok), (0, 0))
        head_rows = lambda a: jnp.pad(a, pad_t).reshape(bd, new_rows, HEAD_DIM).astype(BF16)
        self.operands = [
            q.reshape(bd, self.n_rows, HEAD_DIM), head_rows(k_new), head_rows(v_new),
            jnp.pad(g_new, pad_t).reshape(bd, 1, new_rows),
            g_past.reshape(bd, self.n_groups, self.pp, page_rows),
        ] + [cache_k] * self.pp + [cache_v] * self.pp
        self.new_rows, self.page_rows = new_rows, page_rows

    def in_specs(self, locate):
        pp, layer = self.pp, self.layer

        def per_seq(*idx):
            return (locate(*idx[:-1])[0], 0, 0)

        def bias(*idx):
            seq, group = locate(*idx[:-1])
            return (seq, group, 0, 0)

        def page(i):
            def index(*idx):
                seq, group = locate(*idx[:-1])
                return (layer, idx[-1][seq, group * pp + i], 0, 0, 0)
            return index

        kv = [pl.BlockSpec((1, 1, PAGE_SIZE, N_HEADS, HEAD_DIM), page(i)) for i in range(pp)]
        return [
            pl.BlockSpec((1, self.n_rows, HEAD_DIM), per_seq),
            pl.BlockSpec((1, self.new_rows, HEAD_DIM), per_seq),
            pl.BlockSpec((1, self.new_rows, HEAD_DIM), per_seq),
            pl.BlockSpec((1, 1, self.new_rows), per_seq),
            pl.BlockSpec((1, 1, pp, self.page_rows), bias),
        ] + kv + kv

    def out_spec(self, locate, seq0):
        return pl.BlockSpec((1, self.n_rows, HEAD_DIM), lambda *idx: (locate(*idx[:-1])[0] - seq0, 0, 0))

    def out_shape(self, n_seq):
        return jax.ShapeDtypeStruct((n_seq, self.n_rows, HEAD_DIM), BF16)

    def scratch_shapes(self):
        return [pltpu.VMEM((self.n_rows, 1), F32), pltpu.VMEM((self.n_rows, 1), F32),
                pltpu.VMEM((self.n_rows, HEAD_DIM), F32)]

    def step_params(self):
        return dict(pages_per_step=self.pp, n_groups=self.n_groups)


def _hosted_kernel(pt_ref, *refs, host_kernel, n_host, n_dec_in, step_params, n_active):
    del pt_ref
    n_in, n_out, n_scratch = n_host
    host_in, refs = refs[:n_in], refs[n_in:]
    dec_in, refs = refs[:n_dec_in], refs[n_dec_in:]
    host_out, refs = refs[:n_out], refs[n_out:]
    dec_out, refs = refs[0], refs[1:]
    host_scratch, state = refs[:n_scratch], refs[n_scratch:]
    lin = pl.program_id(0) * pl.num_programs(1) + pl.program_id(1)
    _decode_step(dec_in, dec_out, state, lin=lin, n_active=n_active, **step_params,
                 host_body=lambda: host_kernel(*host_in, *host_out, *host_scratch))


def _call_with_decode(host_kernel, *, grid, in_specs, out_specs, out_shape, scratch_shapes, operands, name,
                      stream=None, seq0=0, max_seqs=0):
    params = _cparams(("arbitrary", "arbitrary"))
    steps = grid[0] * grid[1]
    n_seqs = 0 if stream is None else min(max_seqs, stream.bd - seq0, steps // stream.n_groups)
    if n_seqs <= 0:
        outs = pl.pallas_call(host_kernel, grid=grid, in_specs=in_specs, out_specs=out_specs, out_shape=out_shape,
                              scratch_shapes=scratch_shapes, compiler_params=params, name=name)(*operands)
        return outs, None, 0
    n_groups = stream.n_groups
    n_active = n_seqs * n_groups

    def locate(i, j):
        lin = jnp.minimum(i * grid[1] + j, n_active - 1)
        return seq0 + lin // n_groups, lin % n_groups

    def with_table(spec):
        return pl.BlockSpec(spec.block_shape, lambda *idx, f=spec.index_map: f(*idx[:-1]))

    dec_in_specs = stream.in_specs(locate)
    kernel_fn = functools.partial(
        _hosted_kernel, host_kernel=host_kernel, n_host=(len(in_specs), len(out_specs), len(scratch_shapes)),
        n_dec_in=len(dec_in_specs), step_params=stream.step_params(), n_active=None if n_active == steps else n_active)
    outs = pl.pallas_call(
        kernel_fn,
        grid_spec=pltpu.PrefetchScalarGridSpec(
            num_scalar_prefetch=1,
            grid=grid,
            in_specs=[with_table(s) for s in in_specs] + dec_in_specs,
            out_specs=[with_table(s) for s in out_specs] + [stream.out_spec(locate, seq0)],
            scratch_shapes=list(scratch_shapes) + stream.scratch_shapes(),
        ),
        out_shape=list(out_shape) + [stream.out_shape(n_seqs)],
        compiler_params=params,
        name=name + "_with_decode",
    )(stream.page_table, *operands, *stream.operands)
    return outs[:-1], outs[-1], n_seqs


def _attn_sample_kernel(pt_ref, *refs, n_dec_in, step_params):
    del pt_ref
    dec_in, o_ref, state = refs[:n_dec_in], refs[n_dec_in], refs[n_dec_in + 1:]
    _decode_step(dec_in, o_ref, state, lin=pl.program_id(0) * pl.num_programs(1) + pl.program_id(1),
                 n_active=None, **step_params)


def _attn_sample(stream, seq0):
    n_seqs = stream.bd - seq0
    locate = lambda b, j: (seq0 + b, j)
    dec_in_specs = stream.in_specs(locate)
    return pl.pallas_call(
        functools.partial(_attn_sample_kernel, n_dec_in=len(dec_in_specs), step_params=stream.step_params()),
        grid_spec=pltpu.PrefetchScalarGridSpec(
            num_scalar_prefetch=1,
            grid=(n_seqs, stream.n_groups),
            in_specs=dec_in_specs,
            out_specs=stream.out_spec(locate, seq0),
            scratch_shapes=stream.scratch_shapes(),
        ),
        out_shape=stream.out_shape(n_seqs),
        compiler_params=_cparams(("arbitrary", "arbitrary")),
        name="attn_sample",
    )(stream.page_table, *stream.operands)


def kernel(x_prompt, x_sample, cache_k, cache_v, cache_logf, state_conv, page_table, meta_tokens,
           w_ffn1_in, w_ffn1_out, ln1_g, ln1_b, w_in, b_in, conv_w, conv_b, conv_ln_g, conv_ln_b,
           w_conv_out, w_att_proj, w_o, ln2_g, ln2_b, w_ffn2_in, w_ffn2_out, ln3_g, ln3_b):
    depth = w_ffn1_in.shape[0]
    assert depth == 1, "the meta rows are carried through a single layer only"
    bsz, seq, d = x_prompt.shape
    bd, n_tok, _ = x_sample.shape
    d_att = N_HEADS * HEAD_DIM
    d_conv = conv_w.shape[2]
    alpha = (2.0 * depth) ** 0.25
    o_q = 2 * d_conv
    o_k, o_v, o_f = o_q + d_att, o_q + 2 * d_att, o_q + 3 * d_att
    o_gc = o_f + FORGET_SPLIT
    o_ga = o_gc + d
    row = lambda a: a.reshape(1, -1)

    xp = x_prompt.reshape(bsz * seq, d)
    xs = jnp.concatenate([meta_tokens.astype(F32), x_sample.reshape(bd * n_tok, d)], axis=0)
    outs = [[] for _ in range(8)]
    for l in range(depth):
        w1i, w1o = w_ffn1_in[l].astype(BF16), w_ffn1_out[l].astype(BF16)
        w2i, w2o = w_ffn2_in[l].astype(BF16), w_ffn2_out[l].astype(BF16)
        wl, bl = w_in[l], b_in[l]
        w_u, b_u = wl[:, :o_q].astype(BF16), row(bl[:o_q])
        w_q, b_q = wl[:, o_q:o_k].astype(BF16), row(bl[o_q:o_k])
        w_k, b_k = wl[:, o_k:o_v].astype(BF16), row(bl[o_k:o_v])
        w_v, b_v = wl[:, o_v:o_f].astype(BF16), row(bl[o_v:o_f])
        w_f, b_f = wl[:, o_f:o_gc].astype(BF16), row(bl[o_f:o_gc])
        w_gc, b_gc = wl[:, o_gc:o_ga].astype(BF16), row(bl[o_gc:o_ga])
        w_ga, b_ga = wl[:, o_ga:].astype(BF16), row(bl[o_ga:])
        w_co, w_ap, w_oo = w_conv_out[l].astype(BF16), w_att_proj[l].astype(BF16), w_o[l].astype(BF16)
        g1, b1, g2, b2, g3, b3 = row(ln1_g[l]), row(ln1_b[l]), row(ln2_g[l]), row(ln2_b[l]), row(ln3_g[l]), row(ln3_b[l])
        cb, cg, cbb = row(conv_b[l]), row(conv_ln_g[l]), row(conv_ln_b[l])

        xp, xp_b = _ffn(xp, w1i, w1o, g1, b1, alpha=alpha, emit_bf16=True)
        xs, xs_b = _ffn(xs, w1i, w1o, g1, b1, alpha=alpha, emit_bf16=True)

        xs_b3 = xs_b[None]
        u_s = _proj(xs_b3, w_u, b_u, mode="glu", out_dtype=F32)[0]
        q_s = _proj(xs_b3, w_q, b_q, mode="query", out_dtype=BF16)[0]
        k_s = _proj(xs_b3, w_k, b_k, mode="linear", out_dtype=F32)[0]
        v_s = _proj(xs_b3, w_v, b_v, mode="linear", out_dtype=F32)[0]
        f_s = _proj(xs_b3, w_f, b_f, mode="log_sigmoid", out_dtype=F32)[0]

        xp_b3 = xp_b.reshape(bsz, seq, d)
        u_pre = jnp.concatenate([jnp.zeros((N_META, d_conv), F32), u_s[:N_META]], axis=0)
        u_p = _proj(xp_b3, w_u, b_u, mode="glu", out_dtype=F32, prefix=u_pre)
        q_p = _proj(xp_b3, w_q, b_q, mode="query", out_dtype=BF16)
        k_p = _proj(xp_b3, w_k, b_k, mode="linear", out_dtype=F32, prefix=k_s[:N_META])
        v_p = _proj(xp_b3, w_v, b_v, mode="linear", out_dtype=F32, prefix=v_s[:N_META])
        f_p = _proj(xp_b3, w_f, b_f, mode="log_sigmoid", out_dtype=F32, prefix=f_s[:N_META])

        length = N_META + seq
        f_lanes = f_p.transpose(1, 0, 2).reshape(length, bsz * N_HEADS)
        dk = _cumsum_rows(f_lanes)[0].reshape(length, bsz, N_HEADS).transpose(1, 2, 0)
        dk_meta = dk[:, :, None, :N_META]
        dk_real = dk[:, :, None, N_META:]

        sl = slice(N_META, None)
        n_pages = page_table.shape[1]
        past = n_pages * PAGE_SIZE
        f_new = jnp.pad(f_s[sl].reshape(bd, n_tok, N_HEADS), ((0, 0), (0, CUMSUM_ROW_ALIGN - n_tok), (0, 0)))
        f_new = f_new.transpose(1, 0, 2).reshape(CUMSUM_ROW_ALIGN, bd * N_HEADS)
        g_past, g_new = _cumsum_rows(_gather_logf_pages(cache_logf[l], page_table), f_new)
        g_past = g_past.reshape(past, bd, N_HEADS).transpose(1, 0, 2)
        g_new = g_new.reshape(CUMSUM_ROW_ALIGN, bd, N_HEADS).transpose(1, 0, 2)[:, :n_tok]
        stream = _DecodeStream(
            q_s[sl].reshape(bd, n_tok, d_att), k_s[sl].reshape(bd, n_tok, d_att), v_s[sl].reshape(bd, n_tok, d_att),
            g_new, g_past.reshape(bd, n_pages, PAGE_SIZE, N_HEADS),
            cache_k, cache_v, l, page_table, DECODE_PAGES_PER_STEP)

        att_p, att_s0, n0 = _attn_prompt(q_p, k_p, v_p, dk_meta, dk_real,
                                         stream=stream, seq0=0, max_seqs=DECODE_SEQS_IN_ATTN)
        hc_p, att_s1, n1 = _conv_branch(u_p, conv_w[l], cb, cg, cbb, n_out=seq, off=2,
                                        stream=stream, seq0=n0, max_seqs=DECODE_SEQS_IN_CONV)
        att_parts = [a for a in (att_s0, att_s1) if a is not None]
        if n0 + n1 < bd:
            att_parts.append(_attn_sample(stream, n0 + n1))
        att_s = jnp.concatenate(att_parts, axis=0)
        m_p = _merge(xp_b, hc_p.reshape(bsz * seq, d_conv), att_p.reshape(bsz * seq, d_att),
                     w_gc, w_ga, w_co, w_ap, b_gc, b_ga)
        xp = _mix_out(xp, m_p, w_oo, g2, b2, alpha=alpha)

        u_new = u_s[sl].reshape(bd, n_tok, d_conv)
        hist = jnp.concatenate([state_conv[l].astype(F32), u_new], axis=1)
        hc_s, _, _ = _conv_branch(hist, conv_w[l], cb, cg, cbb, n_out=n_tok, off=0)
        xs_s = xs[sl]
        m_s = _merge(xs_b[sl], hc_s.reshape(bd * n_tok, d_conv), att_s.reshape(bd * n_tok, d_att),
                     w_gc, w_ga, w_co, w_ap, b_gc, b_ga)
        xs_s = _mix_out(xs_s, m_s, w_oo, g2, b2, alpha=alpha)

        xp = _ffn(xp, w2i, w2o, g3, b3, alpha=alpha, emit_bf16=False)
        xs_s = _ffn(xs_s, w2i, w2o, g3, b3, alpha=alpha, emit_bf16=False)
        xs = jnp.concatenate([meta_tokens.astype(F32), xs_s], axis=0)

        hd_shape = (N_HEADS, HEAD_DIM)
        outs[0].append(k_p.reshape(bsz, length, *hd_shape))
        outs[1].append(v_p.reshape(bsz, length, *hd_shape))
        outs[2].append(f_p)
        outs[3].append(u_p[:, -(CONV_WIDTH - 1):])
        outs[4].append(k_s[sl].reshape(bd, n_tok, *hd_shape))
        outs[5].append(v_s[sl].reshape(bd, n_tok, *hd_shape))
        outs[6].append(f_s[sl].reshape(bd, n_tok, N_HEADS))
        outs[7].append(hist[:, -(CONV_WIDTH - 1):])

    y_prompt = xp.reshape(bsz, seq, d)
    y_sample = xs[N_META:].reshape(bd, n_tok, d)
    return (y_prompt, y_sample) + tuple(jnp.stack(o, axis=0) for o in outs)
```

```python
import functools
import math

import jax
import jax.numpy as jnp
from jax import lax
from jax.experimental import pallas as pl
from jax.experimental.pallas import tpu as pltpu

F32 = jnp.float32
BF16 = jnp.bfloat16

N_META = 16
N_HEADS = 16
HEAD_DIM = 128
CONV_WIDTH = 31
PAGE_SIZE = 128
FORGET_SPLIT = N_HEADS
LN_EPS = 1e-5
NEG_INF = -1e30

V7X_VMEM_LIMIT_BYTES = 56 * 1024 * 1024
V7X_LANES = 128
V7X_SUBLANES = 8
NEW_TOKENS = 8
CUMSUM_ROW_ALIGN = 16
DECODE_PAGES_PER_STEP = 8
DECODE_SEQS_IN_ATTN = 16
DECODE_SEQS_IN_CONV = 16
CONV_ROW_TILE = 64


def _cparams(semantics):
    return pltpu.CompilerParams(dimension_semantics=semantics, vmem_limit_bytes=V7X_VMEM_LIMIT_BYTES)


def _row_tile(rows, preferred):
    best = None
    for t in range(16, min(rows, preferred) + 1, 16):
        if rows % t == 0:
            best = t
    return best if best is not None else rows


def _layer_norm(y, g, b):
    mu = jnp.mean(y, axis=-1, keepdims=True)
    d = y - mu
    var = jnp.mean(d * d, axis=-1, keepdims=True)
    return d * lax.rsqrt(var + LN_EPS) * g + b


def _log_sigmoid(z):
    return jnp.minimum(z, 0.0) - jnp.log1p(jnp.exp(-jnp.abs(z)))


def _split3(x):
    hi = x.astype(BF16)
    r1 = x - hi.astype(F32)
    mid = r1.astype(BF16)
    lo = (r1 - mid.astype(F32)).astype(BF16)
    return hi, mid, lo


def _dot01(sel, x):
    hi, mid, lo = _split3(x)
    out = jnp.dot(sel, lo, preferred_element_type=F32)
    out = out + jnp.dot(sel, mid, preferred_element_type=F32)
    return out + jnp.dot(sel, hi, preferred_element_type=F32)


def _ffn_kernel(x_ref, wa_ref, wb_ref, wo_ref, g_ref, b_ref, *rest, alpha, n_f, emit_bf16):
    if emit_bf16:
        o_ref, ob_ref, xb_ref, acc_ref = rest
    else:
        o_ref, xb_ref, acc_ref = rest
    f = pl.program_id(1)

    @pl.when(f == 0)
    def _init():
        xb_ref[...] = x_ref[...].astype(BF16)
        acc_ref[...] = jnp.zeros_like(acc_ref)

    xb = xb_ref[...]
    a = jnp.dot(xb, wa_ref[...], preferred_element_type=F32)
    b = jnp.dot(xb, wb_ref[...], preferred_element_type=F32)
    h = (a * jax.nn.sigmoid(a) * b).astype(BF16)
    acc_ref[...] += jnp.dot(h, wo_ref[...], preferred_element_type=F32)

    @pl.when(f == n_f - 1)
    def _finish():
        y = alpha * x_ref[...] + 0.5 * acc_ref[...]
        o = _layer_norm(y, g_ref[...], b_ref[...])
        o_ref[...] = o
        if emit_bf16:
            ob_ref[...] = o.astype(BF16)


def _ffn(x, w_in, w_out, g, b, *, alpha, emit_bf16, tm_pref=512, tf=512):
    rows, d = x.shape
    d_ff = w_out.shape[0]
    tm = _row_tile(rows, tm_pref)
    n_f = d_ff // tf
    out_shape = [jax.ShapeDtypeStruct((rows, d), F32)]
    out_specs = [pl.BlockSpec((tm, d), lambda i, f: (i, 0))]
    if emit_bf16:
        out_shape.append(jax.ShapeDtypeStruct((rows, d), BF16))
        out_specs.append(pl.BlockSpec((tm, d), lambda i, f: (i, 0)))
    res = pl.pallas_call(
        functools.partial(_ffn_kernel, alpha=alpha, n_f=n_f, emit_bf16=emit_bf16),
        grid=(rows // tm, n_f),
        in_specs=[
            pl.BlockSpec((tm, d), lambda i, f: (i, 0)),
            pl.BlockSpec((d, tf), lambda i, f: (0, f)),
            pl.BlockSpec((d, tf), lambda i, f: (0, f + n_f)),
            pl.BlockSpec((tf, d), lambda i, f: (f, 0)),
            pl.BlockSpec((1, d), lambda i, f: (0, 0)),
            pl.BlockSpec((1, d), lambda i, f: (0, 0)),
        ],
        out_specs=out_specs,
        out_shape=out_shape,
        scratch_shapes=[pltpu.VMEM((tm, d), BF16), pltpu.VMEM((tm, d), F32)],
        compiler_params=_cparams(("parallel", "arbitrary")),
        name="ffn_sublayer",
    )(x, w_in, w_in, w_out, g, b)
    return res if emit_bf16 else res[0]


def _proj_kernel(x_ref, *refs, mode, n_prefix):
    x = x_ref[0]
    if mode == "glu":
        wa_ref, wg_ref, ba_ref, bg_ref = refs[:4]
        refs = refs[4:]
        za = jnp.dot(x, wa_ref[...], preferred_element_type=F32) + ba_ref[...]
        zg = jnp.dot(x, wg_ref[...], preferred_element_type=F32) + bg_ref[...]
        z = za * jax.nn.sigmoid(zg)
    else:
        w_ref, b_ref = refs[:2]
        refs = refs[2:]
        z = jnp.dot(x, w_ref[...], preferred_element_type=F32) + b_ref[...]
        if mode == "log_sigmoid":
            z = _log_sigmoid(z)
        elif mode == "query":
            z = z * HEAD_DIM ** -0.5
    if n_prefix:
        pre_ref, o_ref = refs
        o_ref[0, :n_prefix, :] = pre_ref[...].astype(o_ref.dtype)
        o_ref[0, n_prefix:, :] = z.astype(o_ref.dtype)
    else:
        (o_ref,) = refs
        o_ref[0] = z.astype(o_ref.dtype)


def _proj(x, w, bias, *, mode, out_dtype, prefix=None, tn=512):
    bsz, rows, d = x.shape
    n = w.shape[1] // 2 if mode == "glu" else w.shape[1]
    tn = min(tn, n)
    n_t = n // tn
    n_prefix = 0 if prefix is None else prefix.shape[0]
    in_specs = [pl.BlockSpec((1, rows, d), lambda b, j: (b, 0, 0))]
    args = [x]
    if mode == "glu":
        in_specs += [
            pl.BlockSpec((d, tn), lambda b, j: (0, j)),
            pl.BlockSpec((d, tn), lambda b, j: (0, j + n_t)),
            pl.BlockSpec((1, tn), lambda b, j: (0, j)),
            pl.BlockSpec((1, tn), lambda b, j: (0, j + n_t)),
        ]
        args += [w, w, bias, bias]
    else:
        in_specs += [pl.BlockSpec((d, tn), lambda b, j: (0, j)), pl.BlockSpec((1, tn), lambda b, j: (0, j))]
        args += [w, bias]
    if n_prefix:
        in_specs.append(pl.BlockSpec((n_prefix, tn), lambda b, j: (0, j)))
        args.append(prefix)
    return pl.pallas_call(
        functools.partial(_proj_kernel, mode=mode, n_prefix=n_prefix),
        grid=(bsz, n_t),
        in_specs=in_specs,
        out_specs=pl.BlockSpec((1, n_prefix + rows, tn), lambda b, j: (b, 0, j)),
        out_shape=jax.ShapeDtypeStruct((bsz, n_prefix + rows, n), out_dtype),
        compiler_params=_cparams(("parallel", "arbitrary")),
        name="in_proj_" + mode,
    )(*args)


def _cumsum_kernel(*refs, chunk):
    n_seg = len(refs) // 2
    lanes = refs[0].shape[1]
    r = lax.broadcasted_iota(jnp.int32, (chunk, chunk), 0)
    c = lax.broadcasted_iota(jnp.int32, (chunk, chunk), 1)
    tri = jnp.where(c <= r, 1.0, 0.0).astype(BF16)
    carry = jnp.zeros((1, lanes), F32)
    for x_ref, o_ref in zip(refs[:n_seg], refs[n_seg:]):
        length = x_ref.shape[0]
        for start in range(0, length, chunk):
            size = min(chunk, length - start)
            local = _dot01(tri[:size, :size], x_ref[start:start + size, :])
            o_ref[start:start + size, :] = local + carry
            carry = carry + local[size - 1:size, :]


def _cumsum_rows(*segments, chunk=256):
    lanes = segments[0].shape[1]
    tl = V7X_LANES if lanes % V7X_LANES == 0 else lanes
    specs = [pl.BlockSpec((s.shape[0], tl), lambda i: (0, i)) for s in segments]
    return pl.pallas_call(
        functools.partial(_cumsum_kernel, chunk=chunk),
        grid=(lanes // tl,),
        in_specs=specs,
        out_specs=specs,
        out_shape=[jax.ShapeDtypeStruct(s.shape, F32) for s in segments],
        compiler_params=_cparams(("parallel",)),
        name="logf_cumsum",
    )(*segments)


def _attn_prompt_tiles(q_ref, k_ref, v_ref, dkm_ref, dkr_ref, o_ref, *, tq, tiles):
    n_max = N_META + (max(tiles) + 1) * tq
    kb = k_ref[0, :n_max, :].astype(BF16)
    vb = v_ref[0, :n_max, :].astype(BF16)
    k_meta, v_meta = kb[:N_META], vb[:N_META]
    bias_meta = -dkm_ref[0, 0]
    contract_last = (((1,), (1,)), ((), ()))
    causal = lax.broadcasted_iota(jnp.int32, (tq, tq), 1) <= lax.broadcasted_iota(jnp.int32, (tq, tq), 0)
    for qi in tiles:
        lo, hi = N_META + qi * tq, N_META + (qi + 1) * tq
        q = q_ref[0, qi * tq:(qi + 1) * tq, :]
        blocks = [(lax.dot_general(q, k_meta, contract_last, preferred_element_type=F32) + bias_meta, v_meta)]
        s_own = lax.dot_general(q, kb[lo:hi], contract_last, preferred_element_type=F32)
        s_own = s_own - dkr_ref[0, 0, :, qi * tq:(qi + 1) * tq]
        blocks.append((jnp.where(causal, s_own, NEG_INF), vb[lo:hi]))
        if qi:
            s_past = lax.dot_general(q, kb[N_META:lo], contract_last, preferred_element_type=F32)
            blocks.append((s_past - dkr_ref[0, 0, :, :qi * tq], vb[N_META:lo]))
        m = functools.reduce(jnp.maximum, [jnp.max(s, axis=-1, keepdims=True) for s, _ in blocks])
        denom, o = 0.0, 0.0
        for s, v in blocks:
            p = jnp.exp(s - m)
            denom = denom + jnp.sum(p, axis=-1, keepdims=True)
            o = o + jnp.dot(p.astype(BF16), v, preferred_element_type=F32)
        o_ref[0, qi * tq:(qi + 1) * tq, :] = (o / denom).astype(o_ref.dtype)


def _attn_prompt_kernel(*refs, tq, tile_groups):
    if len(tile_groups) == 1:
        _attn_prompt_tiles(*refs, tq=tq, tiles=tile_groups[0])
        return
    part = pl.program_id(1) % len(tile_groups)
    for i, tiles in enumerate(tile_groups):
        pl.when(part == i)(functools.partial(_attn_prompt_tiles, *refs, tq=tq, tiles=tiles))


def _attn_prompt(q, k, v, dk_meta, dk_real, *, tq_pref=256, **decode):
    bsz, seq, _ = q.shape
    length = k.shape[1]
    tq = _row_tile(seq, tq_pref)
    n_tiles = seq // tq
    if n_tiles % 4 == 0:
        tile_groups = [[], []]
        for i in range(n_tiles // 2):
            tile_groups[i % 2] += [i, n_tiles - 1 - i]
    elif n_tiles % 2 == 0:
        tile_groups = [list(range(n_tiles // 2)), list(range(n_tiles // 2, n_tiles))]
    else:
        tile_groups = [list(range(n_tiles))]
    assert sorted(t for g in tile_groups for t in g) == list(range(n_tiles))
    ns = len(tile_groups)
    (out,), dec_out, n_seqs = _call_with_decode(
        functools.partial(_attn_prompt_kernel, tq=tq, tile_groups=tile_groups),
        grid=(bsz, N_HEADS * ns),
        in_specs=[
            pl.BlockSpec((1, seq, HEAD_DIM), lambda b, g: (b, 0, g // ns)),
            pl.BlockSpec((1, length, HEAD_DIM), lambda b, g: (b, 0, g // ns)),
            pl.BlockSpec((1, length, HEAD_DIM), lambda b, g: (b, 0, g // ns)),
            pl.BlockSpec((1, 1, 1, N_META), lambda b, g: (b, g // ns, 0, 0)),
            pl.BlockSpec((1, 1, 1, seq), lambda b, g: (b, g // ns, 0, 0)),
        ],
        out_specs=[pl.BlockSpec((1, seq, HEAD_DIM), lambda b, g: (b, 0, g // ns))],
        out_shape=[jax.ShapeDtypeStruct(q.shape, BF16)],
        scratch_shapes=[],
        operands=(q, k, v, dk_meta, dk_real),
        name="attn_prompt",
        **decode,
    )
    return out, dec_out, n_seqs


def _conv_taps_by_shift(window, cw_ref, r0, c0, off, n_rows):
    sub = V7X_SUBLANES
    n_out = n_rows // sub
    n_win = -(-(n_rows + off + CONV_WIDTH - 1) // sub)
    win = window[r0:r0 + n_win * sub, c0:c0 + V7X_LANES].reshape(n_win, sub, V7X_LANES)
    sublane = lax.broadcasted_iota(jnp.int32, (1, sub, V7X_LANES), 1)
    out = jnp.zeros((n_out, sub, V7X_LANES), F32)
    for shift in range(sub):
        taps = [w for w in range(CONV_WIDTH) if (off + w) % sub == shift]
        if not taps:
            continue
        rotated = win if shift == 0 else pltpu.roll(win, sub - shift, axis=1)
        n_blocks = n_out if shift == 0 else n_out + 1
        part = jnp.zeros((n_blocks, sub, V7X_LANES), F32)
        for w in taps:
            a = (off + w) // sub
            part = part + cw_ref[w:w + 1, c0:c0 + V7X_LANES][None] * rotated[a:a + n_blocks]
        if shift == 0:
            out = out + part
        else:
            out = out + jnp.where(sublane < sub - shift, part[:n_out], part[1:])
    return out.reshape(n_rows, V7X_LANES)


def _conv_kernel(u_ref, cw_ref, cb_ref, g_ref, b_ref, o_ref, h_ref, *, tr, off, sub_rows):
    channels = o_ref.shape[2]
    base = pl.multiple_of(pl.program_id(1) * tr, V7X_SUBLANES) if tr % V7X_SUBLANES == 0 else 0
    window = u_ref.at[0, pl.ds(base, tr + off + CONV_WIDTH - 1), :]
    for c0 in range(0, channels, V7X_LANES):
        for r0 in range(0, tr, sub_rows):
            if sub_rows % V7X_SUBLANES == 0:
                acc = _conv_taps_by_shift(window, cw_ref, r0, c0, off, sub_rows)
            else:
                acc = jnp.zeros((sub_rows, V7X_LANES), F32)
                for w in range(CONV_WIDTH):
                    rows = window[r0 + off + w:r0 + off + w + sub_rows, c0:c0 + V7X_LANES]
                    acc = acc + cw_ref[w:w + 1, c0:c0 + V7X_LANES] * rows
            h_ref[r0:r0 + sub_rows, c0:c0 + V7X_LANES] = acc
    hn = _layer_norm(h_ref[...] + cb_ref[...], g_ref[...], b_ref[...])
    o_ref[0] = (hn * jax.nn.sigmoid(hn)).astype(o_ref.dtype)


def _conv_branch(hist, conv_w, conv_b, ln_g, ln_b, *, n_out, off, tr_pref=CONV_ROW_TILE, **decode):
    bsz, hist_rows, channels = hist.shape
    tr = _row_tile(n_out, tr_pref) if n_out % 16 == 0 else n_out
    assert hist_rows >= n_out + off + CONV_WIDTH - 1
    sub_rows = math.gcd(tr, 64)
    (out,), dec_out, n_seqs = _call_with_decode(
        functools.partial(_conv_kernel, tr=tr, off=off, sub_rows=sub_rows),
        grid=(bsz, n_out // tr),
        in_specs=[
            pl.BlockSpec((1, hist_rows, channels), lambda b, r: (b, 0, 0)),
            pl.BlockSpec((CONV_WIDTH, channels), lambda b, r: (0, 0)),
            pl.BlockSpec((1, channels), lambda b, r: (0, 0)),
            pl.BlockSpec((1, channels), lambda b, r: (0, 0)),
            pl.BlockSpec((1, channels), lambda b, r: (0, 0)),
        ],
        out_specs=[pl.BlockSpec((1, tr, channels), lambda b, r: (b, r, 0))],
        out_shape=[jax.ShapeDtypeStruct((bsz, n_out, channels), BF16)],
        scratch_shapes=[pltpu.VMEM((tr, channels), F32)],
        operands=(hist, conv_w, conv_b, ln_g, ln_b),
        name="conv_branch",
        **decode,
    )
    return out, dec_out, n_seqs


def _merge_kernel(x_ref, hc_ref, att_ref, wgc_ref, wga_ref, wco_ref, wap_ref, bgc_ref, bga_ref, o_ref):
    x = x_ref[...]
    g_conv = jax.nn.sigmoid(jnp.dot(x, wgc_ref[...], preferred_element_type=F32) + bgc_ref[...])
    g_att = jax.nn.sigmoid(jnp.dot(x, wga_ref[...], preferred_element_type=F32) + bga_ref[...])
    conv_d = jnp.dot(hc_ref[...], wco_ref[...], preferred_element_type=F32)
    att_d = jnp.dot(att_ref[...], wap_ref[...], preferred_element_type=F32)
    o_ref[...] = (g_conv * conv_d + g_att * att_d).astype(o_ref.dtype)


def _merge(xb, h_conv, att, w_gc, w_ga, w_co, w_ap, b_gc, b_ga, *, tm_pref=1024, tn=512):
    rows, d = xb.shape
    d_conv = h_conv.shape[1]
    d_att = att.shape[1]
    tm = _row_tile(rows, tm_pref)
    return pl.pallas_call(
        _merge_kernel,
        grid=(rows // tm, d // tn),
        in_specs=[
            pl.BlockSpec((tm, d), lambda i, j: (i, 0)),
            pl.BlockSpec((tm, d_conv), lambda i, j: (i, 0)),
            pl.BlockSpec((tm, d_att), lambda i, j: (i, 0)),
            pl.BlockSpec((d, tn), lambda i, j: (0, j)),
            pl.BlockSpec((d, tn), lambda i, j: (0, j)),
            pl.BlockSpec((d_conv, tn), lambda i, j: (0, j)),
            pl.BlockSpec((d_att, tn), lambda i, j: (0, j)),
            pl.BlockSpec((1, tn), lambda i, j: (0, j)),
            pl.BlockSpec((1, tn), lambda i, j: (0, j)),
        ],
        out_specs=pl.BlockSpec((tm, tn), lambda i, j: (i, j)),
        out_shape=jax.ShapeDtypeStruct((rows, d), BF16),
        compiler_params=_cparams(("parallel", "arbitrary")),
        name="branch_merge",
    )(xb, h_conv, att, w_gc, w_ga, w_co, w_ap, b_gc, b_ga)


def _mix_out_kernel(x_ref, m_ref, wo_ref, g_ref, b_ref, o_ref, *, alpha):
    y = alpha * x_ref[...] + jnp.dot(m_ref[...], wo_ref[...], preferred_element_type=F32)
    o_ref[...] = _layer_norm(y, g_ref[...], b_ref[...])


def _mix_out(x, m, w_o, g, b, *, alpha, tm_pref=512):
    rows, d = x.shape
    tm = _row_tile(rows, tm_pref)
    return pl.pallas_call(
        functools.partial(_mix_out_kernel, alpha=alpha),
        grid=(rows // tm,),
        in_specs=[
            pl.BlockSpec((tm, d), lambda i: (i, 0)),
            pl.BlockSpec((tm, d), lambda i: (i, 0)),
            pl.BlockSpec((d, d), lambda i: (0, 0)),
            pl.BlockSpec((1, d), lambda i: (0, 0)),
            pl.BlockSpec((1, d), lambda i: (0, 0)),
        ],
        out_specs=pl.BlockSpec((tm, d), lambda i: (i, 0)),
        out_shape=jax.ShapeDtypeStruct((rows, d), F32),
        compiler_params=_cparams(("parallel",)),
        name="mixer_out",
    )(x, m, w_o, g, b)


def _gather_pages_kernel(pt_ref, *refs, pages_per_step, seqs_per_step):
    del pt_ref
    o_ref = refs[-1]
    page = refs[0].shape[1]
    for p in range(pages_per_step):
        pieces = [refs[s * pages_per_step + p][0] for s in range(seqs_per_step)]
        o_ref[p * page:(p + 1) * page, :] = jnp.concatenate(pieces, axis=1)


def _gather_logf_pages(cache_logf, page_table, *, pages_per_step=8):
    bd, n_pages = page_table.shape
    _, page, heads = cache_logf.shape
    pp = math.gcd(n_pages, pages_per_step)
    sps = math.gcd(bd, V7X_LANES // heads)
    if sps * heads != V7X_LANES:
        sps = bd

    def page_map(s, p):
        return lambda g, j, pt: (pt[g * sps + s, j * pp + p], 0, 0)

    return pl.pallas_call(
        functools.partial(_gather_pages_kernel, pages_per_step=pp, seqs_per_step=sps),
        grid_spec=pltpu.PrefetchScalarGridSpec(
            num_scalar_prefetch=1,
            grid=(bd // sps, n_pages // pp),
            in_specs=[pl.BlockSpec((1, page, heads), page_map(s, p)) for s in range(sps) for p in range(pp)],
            out_specs=pl.BlockSpec((pp * page, sps * heads), lambda g, j, pt: (j, g)),
        ),
        out_shape=jax.ShapeDtypeStruct((n_pages * page, bd * heads), F32),
        compiler_params=_cparams(("parallel", "arbitrary")),
        name="gather_logf_pages",
    )(page_table, *([cache_logf] * (sps * pp)))


_CONTRACT_LAST = (((1,), (1,)), ((), ()))


def _same_head(n_rows, n_cols):
    row_head = lax.broadcasted_iota(jnp.int32, (n_rows, n_cols), 0) % N_HEADS
    col_head = lax.broadcasted_iota(jnp.int32, (n_rows, n_cols), 1) % N_HEADS
    return row_head == col_head


def _decode_update(state, scores, values):
    m_ref, l_ref, acc_ref = state
    m_old = m_ref[...]
    m_new = m_old
    for s in scores:
        m_new = jnp.maximum(m_new, jnp.max(s, axis=-1, keepdims=True))
    corr = jnp.exp(m_old - m_new)
    l_new = l_ref[...] * corr
    acc = acc_ref[...] * corr
    for s, v in zip(scores, values):
        p = jnp.exp(s - m_new)
        l_new = l_new + jnp.sum(p, axis=-1, keepdims=True)
        acc = acc + jnp.dot(p.astype(BF16), v, preferred_element_type=F32)
    l_ref[...] = l_new
    acc_ref[...] = acc
    m_ref[...] = m_new


def _decode_begin(q_ref, kn_ref, vn_ref, gn_ref, state):
    m_ref, l_ref, acc_ref = state
    q = q_ref[0]
    n_rows = q.shape[0]
    n_cols = NEW_TOKENS * N_HEADS
    s = lax.dot_general(q, kn_ref[0], _CONTRACT_LAST, preferred_element_type=F32) - gn_ref[0]
    key = lax.broadcasted_iota(jnp.int32, (n_rows, n_cols), 1) // N_HEADS
    tok = lax.broadcasted_iota(jnp.int32, (n_rows, n_cols), 0) // N_HEADS
    s = jnp.where(_same_head(n_rows, n_cols) & (key <= tok), s, NEG_INF)
    m_ref[...] = jnp.full_like(m_ref, -jnp.inf)
    l_ref[...] = jnp.zeros_like(l_ref)
    acc_ref[...] = jnp.zeros_like(acc_ref)
    _decode_update(state, [s], [vn_ref[0]])


def _decode_pages(q_ref, g_ref, k_refs, v_refs, state):
    q = q_ref[0]
    page_rows = PAGE_SIZE * N_HEADS
    page_mask = _same_head(q.shape[0], page_rows)
    scores, values = [], []
    for i, (k_ref, v_ref) in enumerate(zip(k_refs, v_refs)):
        k2 = k_ref[0, 0].reshape(page_rows, HEAD_DIM).astype(BF16)
        s = lax.dot_general(q, k2, _CONTRACT_LAST, preferred_element_type=F32) - g_ref[0, 0, i:i + 1, :]
        scores.append(jnp.where(page_mask, s, NEG_INF))
        values.append(v_ref[0, 0].reshape(page_rows, HEAD_DIM).astype(BF16))
    _decode_update(state, scores, values)


def _decode_end(o_ref, state):
    _, l_ref, acc_ref = state
    o_ref[0] = (acc_ref[...] / l_ref[...]).astype(o_ref.dtype)


def _decode_step(dec_in, o_ref, state, *, pages_per_step, n_groups, lin, n_active, host_body=None):
    q_ref, kn_ref, vn_ref, gn_ref, g_ref = dec_in[:5]
    k_refs = dec_in[5:5 + pages_per_step]
    v_refs = dec_in[5 + pages_per_step:]
    group = lin % n_groups
    always = n_active is None
    active = None if always else lin < n_active

    def guard(cond):
        return cond if always else cond & active

    @pl.when(guard(group == 0))
    def _begin():
        _decode_begin(q_ref, kn_ref, vn_ref, gn_ref, state)

    if host_body is not None:
        host_body()
    if always:
        _decode_pages(q_ref, g_ref, k_refs, v_refs, state)
    else:
        pl.when(active)(lambda: _decode_pages(q_ref, g_ref, k_refs, v_refs, state))

    @pl.when(guard(group == n_groups - 1))
    def _end():
        _decode_end(o_ref, state)


class _DecodeStream:
    def __init__(self, q, k_new, v_new, g_new, g_past, cache_k, cache_v, layer, page_table, pages_per_step):
        bd, n_tok, d_att = q.shape
        n_pages = page_table.shape[1]
        assert n_tok <= NEW_TOKENS
        self.bd, self.n_tok, self.d_att = bd, n_tok, d_att
        self.pp = math.gcd(n_pages, pages_per_step)
        self.n_groups = n_pages // self.pp
        self.n_rows = n_tok * N_HEADS
        self.layer = layer
        self.page_table = page_table
        new_rows = NEW_TOKENS * N_HEADS
        page_rows = PAGE_SIZE * N_HEADS
        pad_t = ((0, 0), (0, NEW_TOKENS - n_tok), (0, 0))
        head_rows = lambda a: jnp.pad(a, pad_t).reshape(bd, new_rows, HEAD_DIM).astype(BF16)
        self.operands = [
            q.reshape(bd, self.n_rows, HEAD_DIM), head_rows(k_new), head_rows(v_new),
            jnp.pad(g_new, pad_t).reshape(bd, 1, new_rows),
            g_past.reshape(bd, self.n_groups, self.pp, page_rows),
        ] + [cache_k] * self.pp + [cache_v] * self.pp
        self.new_rows, self.page_rows = new_rows, page_rows

    def in_specs(self, locate):
        pp, layer = self.pp, self.layer

        def per_seq(*idx):
            return (locate(*idx[:-1])[0], 0, 0)

        def bias(*idx):
            seq, group = locate(*idx[:-1])
            return (seq, group, 0, 0)

        def page(i):
            def index(*idx):
                seq, group = locate(*idx[:-1])
                return (layer, idx[-1][seq, group * pp + i], 0, 0, 0)
            return index

        kv = [pl.BlockSpec((1, 1, PAGE_SIZE, N_HEADS, HEAD_DIM), page(i)) for i in range(pp)]
        return [
            pl.BlockSpec((1, self.n_rows, HEAD_DIM), per_seq),
            pl.BlockSpec((1, self.new_rows, HEAD_DIM), per_seq),
            pl.BlockSpec((1, self.new_rows, HEAD_DIM), per_seq),
            pl.BlockSpec((1, 1, self.new_rows), per_seq),
            pl.BlockSpec((1, 1, pp, self.page_rows), bias),
        ] + kv + kv

    def out_spec(self, locate, seq0):
        return pl.BlockSpec((1, self.n_rows, HEAD_DIM), lambda *idx: (locate(*idx[:-1])[0] - seq0, 0, 0))

    def out_shape(self, n_seq):
        return jax.ShapeDtypeStruct((n_seq, self.n_rows, HEAD_DIM), BF16)

    def scratch_shapes(self):
        return [pltpu.VMEM((self.n_rows, 1), F32), pltpu.VMEM((self.n_rows, 1), F32),
                pltpu.VMEM((self.n_rows, HEAD_DIM), F32)]

    def step_params(self):
        return dict(pages_per_step=self.pp, n_groups=self.n_groups)


def _hosted_kernel(pt_ref, *refs, host_kernel, n_host, n_dec_in, step_params, n_active):
    del pt_ref
    n_in, n_out, n_scratch = n_host
    host_in, refs = refs[:n_in], refs[n_in:]
    dec_in, refs = refs[:n_dec_in], refs[n_dec_in:]
    host_out, refs = refs[:n_out], refs[n_out:]
    dec_out, refs = refs[0], refs[1:]
    host_scratch, state = refs[:n_scratch], refs[n_scratch:]
    lin = pl.program_id(0) * pl.num_programs(1) + pl.program_id(1)
    _decode_step(dec_in, dec_out, state, lin=lin, n_active=n_active, **step_params,
                 host_body=lambda: host_kernel(*host_in, *host_out, *host_scratch))


def _call_with_decode(host_kernel, *, grid, in_specs, out_specs, out_shape, scratch_shapes, operands, name,
                      stream=None, seq0=0, max_seqs=0):
    params = _cparams(("arbitrary", "arbitrary"))
    steps = grid[0] * grid[1]
    n_seqs = 0 if stream is None else min(max_seqs, stream.bd - seq0, steps // stream.n_groups)
    if n_seqs <= 0:
        outs = pl.pallas_call(host_kernel, grid=grid, in_specs=in_specs, out_specs=out_specs, out_shape=out_shape,
                              scratch_shapes=scratch_shapes, compiler_params=params, name=name)(*operands)
        return outs, None, 0
    n_groups = stream.n_groups
    n_active = n_seqs * n_groups

    def locate(i, j):
        lin = jnp.minimum(i * grid[1] + j, n_active - 1)
        return seq0 + lin // n_groups, lin % n_groups

    def with_table(spec):
        return pl.BlockSpec(spec.block_shape, lambda *idx, f=spec.index_map: f(*idx[:-1]))

    dec_in_specs = stream.in_specs(locate)
    kernel_fn = functools.partial(
        _hosted_kernel, host_kernel=host_kernel, n_host=(len(in_specs), len(out_specs), len(scratch_shapes)),
        n_dec_in=len(dec_in_specs), step_params=stream.step_params(), n_active=None if n_active == steps else n_active)
    outs = pl.pallas_call(
        kernel_fn,
        grid_spec=pltpu.PrefetchScalarGridSpec(
            num_scalar_prefetch=1,
            grid=grid,
            in_specs=[with_table(s) for s in in_specs] + dec_in_specs,
            out_specs=[with_table(s) for s in out_specs] + [stream.out_spec(locate, seq0)],
            scratch_shapes=list(scratch_shapes) + stream.scratch_shapes(),
        ),
        out_shape=list(out_shape) + [stream.out_shape(n_seqs)],
        compiler_params=params,
        name=name + "_with_decode",
    )(stream.page_table, *operands, *stream.operands)
    return outs[:-1], outs[-1], n_seqs


def _attn_sample_kernel(pt_ref, *refs, n_dec_in, step_params):
    del pt_ref
    dec_in, o_ref, state = refs[:n_dec_in], refs[n_dec_in], refs[n_dec_in + 1:]
    _decode_step(dec_in, o_ref, state, lin=pl.program_id(0) * pl.num_programs(1) + pl.program_id(1),
                 n_active=None, **step_params)


def _attn_sample(stream, seq0):
    n_seqs = stream.bd - seq0
    locate = lambda b, j: (seq0 + b, j)
    dec_in_specs = stream.in_specs(locate)
    return pl.pallas_call(
        functools.partial(_attn_sample_kernel, n_dec_in=len(dec_in_specs), step_params=stream.step_params()),
        grid_spec=pltpu.PrefetchScalarGridSpec(
            num_scalar_prefetch=1,
            grid=(n_seqs, stream.n_groups),
            in_specs=dec_in_specs,
            out_specs=stream.out_spec(locate, seq0),
            scratch_shapes=stream.scratch_shapes(),
        ),
        out_shape=stream.out_shape(n_seqs),
        compiler_params=_cparams(("arbitrary", "arbitrary")),
        name="attn_sample",
    )(stream.page_table, *stream.operands)


def kernel(x_prompt, x_sample, cache_k, cache_v, cache_logf, state_conv, page_table, meta_tokens,
           w_ffn1_in, w_ffn1_out, ln1_g, ln1_b, w_in, b_in, conv_w, conv_b, conv_ln_g, conv_ln_b,
           w_conv_out, w_att_proj, w_o, ln2_g, ln2_b, w_ffn2_in, w_ffn2_out, ln3_g, ln3_b):
    depth = w_ffn1_in.shape[0]
    assert depth == 1, "the meta rows are carried through a single layer only"
    bsz, seq, d = x_prompt.shape
    bd, n_tok, _ = x_sample.shape
    d_att = N_HEADS * HEAD_DIM
    d_conv = conv_w.shape[2]
    alpha = (2.0 * depth) ** 0.25
    o_q = 2 * d_conv
    o_k, o_v, o_f = o_q + d_att, o_q + 2 * d_att, o_q + 3 * d_att
    o_gc = o_f + FORGET_SPLIT
    o_ga = o_gc + d
    row = lambda a: a.reshape(1, -1)

    xp = x_prompt.reshape(bsz * seq, d)
    xs = jnp.concatenate([meta_tokens.astype(F32), x_sample.reshape(bd * n_tok, d)], axis=0)
    outs = [[] for _ in range(8)]
    for l in range(depth):
        w1i, w1o = w_ffn1_in[l].astype(BF16), w_ffn1_out[l].astype(BF16)
        w2i, w2o = w_ffn2_in[l].astype(BF16), w_ffn2_out[l].astype(BF16)
        wl, bl = w_in[l], b_in[l]
        w_u, b_u = wl[:, :o_q].astype(BF16), row(bl[:o_q])
        w_q, b_q = wl[:, o_q:o_k].astype(BF16), row(bl[o_q:o_k])
        w_k, b_k = wl[:, o_k:o_v].astype(BF16), row(bl[o_k:o_v])
        w_v, b_v = wl[:, o_v:o_f].astype(BF16), row(bl[o_v:o_f])
        w_f, b_f = wl[:, o_f:o_gc].astype(BF16), row(bl[o_f:o_gc])
        w_gc, b_gc = wl[:, o_gc:o_ga].astype(BF16), row(bl[o_gc:o_ga])
        w_ga, b_ga = wl[:, o_ga:].astype(BF16), row(bl[o_ga:])
        w_co, w_ap, w_oo = w_conv_out[l].astype(BF16), w_att_proj[l].astype(BF16), w_o[l].astype(BF16)
        g1, b1, g2, b2, g3, b3 = row(ln1_g[l]), row(ln1_b[l]), row(ln2_g[l]), row(ln2_b[l]), row(ln3_g[l]), row(ln3_b[l])
        cb, cg, cbb = row(conv_b[l]), row(conv_ln_g[l]), row(conv_ln_b[l])

        xp, xp_b = _ffn(xp, w1i, w1o, g1, b1, alpha=alpha, emit_bf16=True)
        xs, xs_b = _ffn(xs, w1i, w1o, g1, b1, alpha=alpha, emit_bf16=True)

        xs_b3 = xs_b[None]
        u_s = _proj(xs_b3, w_u, b_u, mode="glu", out_dtype=F32)[0]
        q_s = _proj(xs_b3, w_q, b_q, mode="query", out_dtype=BF16)[0]
        k_s = _proj(xs_b3, w_k, b_k, mode="linear", out_dtype=F32)[0]
        v_s = _proj(xs_b3, w_v, b_v, mode="linear", out_dtype=F32)[0]
        f_s = _proj(xs_b3, w_f, b_f, mode="log_sigmoid", out_dtype=F32)[0]

        xp_b3 = xp_b.reshape(bsz, seq, d)
        u_pre = jnp.concatenate([jnp.zeros((N_META, d_conv), F32), u_s[:N_META]], axis=0)
        u_p = _proj(xp_b3, w_u, b_u, mode="glu", out_dtype=F32, prefix=u_pre)
        q_p = _proj(xp_b3, w_q, b_q, mode="query", out_dtype=BF16)
        k_p = _proj(xp_b3, w_k, b_k, mode="linear", out_dtype=F32, prefix=k_s[:N_META])
        v_p = _proj(xp_b3, w_v, b_v, mode="linear", out_dtype=F32, prefix=v_s[:N_META])
        f_p = _proj(xp_b3, w_f, b_f, mode="log_sigmoid", out_dtype=F32, prefix=f_s[:N_META])

        length = N_META + seq
        f_lanes = f_p.transpose(1, 0, 2).reshape(length, bsz * N_HEADS)
        dk = _cumsum_rows(f_lanes)[0].reshape(length, bsz, N_HEADS).transpose(1, 2, 0)
        dk_meta = dk[:, :, None, :N_META]
        dk_real = dk[:, :, None, N_META:]

        sl = slice(N_META, None)
        n_pages = page_table.shape[1]
        past = n_pages * PAGE_SIZE
        f_new = jnp.pad(f_s[sl].reshape(bd, n_tok, N_HEADS), ((0, 0), (0, CUMSUM_ROW_ALIGN - n_tok), (0, 0)))
        f_new = f_new.transpose(1, 0, 2).reshape(CUMSUM_ROW_ALIGN, bd * N_HEADS)
        g_past, g_new = _cumsum_rows(_gather_logf_pages(cache_logf[l], page_table), f_new)
        g_past = g_past.reshape(past, bd, N_HEADS).transpose(1, 0, 2)
        g_new = g_new.reshape(CUMSUM_ROW_ALIGN, bd, N_HEADS).transpose(1, 0, 2)[:, :n_tok]
        stream = _DecodeStream(
            q_s[sl].reshape(bd, n_tok, d_att), k_s[sl].reshape(bd, n_tok, d_att), v_s[sl].reshape(bd, n_tok, d_att),
            g_new, g_past.reshape(bd, n_pages, PAGE_SIZE, N_HEADS),
            cache_k, cache_v, l, page_table, DECODE_PAGES_PER_STEP)

        att_p, att_s0, n0 = _attn_prompt(q_p, k_p, v_p, dk_meta, dk_real,
                                         stream=stream, seq0=0, max_seqs=DECODE_SEQS_IN_ATTN)
        hc_p, att_s1, n1 = _conv_branch(u_p, conv_w[l], cb, cg, cbb, n_out=seq, off=2,
                                        stream=stream, seq0=n0, max_seqs=DECODE_SEQS_IN_CONV)
        att_parts = [a for a in (att_s0, att_s1) if a is not None]
        if n0 + n1 < bd:
            att_parts.append(_attn_sample(stream, n0 + n1))
        att_s = jnp.concatenate(att_parts, axis=0)
        m_p = _merge(xp_b, hc_p.reshape(bsz * seq, d_conv), att_p.reshape(bsz * seq, d_att),
                     w_gc, w_ga, w_co, w_ap, b_gc, b_ga)
        xp = _mix_out(xp, m_p, w_oo, g2, b2, alpha=alpha)

        u_new = u_s[sl].reshape(bd, n_tok, d_conv)
        hist = jnp.concatenate([state_conv[l].astype(F32), u_new], axis=1)
        hc_s, _, _ = _conv_branch(hist, conv_w[l], cb, cg, cbb, n_out=n_tok, off=0)
        xs_s = xs[sl]
        m_s = _merge(xs_b[sl], hc_s.reshape(bd * n_tok, d_conv), att_s.reshape(bd * n_tok, d_att),
                     w_gc, w_ga, w_co, w_ap, b_gc, b_ga)
        xs_s = _mix_out(xs_s, m_s, w_oo, g2, b2, alpha=alpha)

        xp = _ffn(xp, w2i, w2o, g3, b3, alpha=alpha, emit_bf16=False)
        xs_s = _ffn(xs_s, w2i, w2o, g3, b3, alpha=alpha, emit_bf16=False)
        xs = jnp.concatenate([meta_tokens.astype(F32), xs_s], axis=0)

        hd_shape = (N_HEADS, HEAD_DIM)
        outs[0].append(k_p.reshape(bsz, length, *hd_shape))
        outs[1].append(v_p.reshape(bsz, length, *hd_shape))
        outs[2].append(f_p)
        outs[3].append(u_p[:, -(CONV_WIDTH - 1):])
        outs[4].append(k_s[sl].reshape(bd, n_tok, *hd_shape))
        outs[5].append(v_s[sl].reshape(bd, n_tok, *hd_shape))
        outs[6].append(f_s[sl].reshape(bd, n_tok, N_HEADS))
        outs[7].append(hist[:, -(CONV_WIDTH - 1):])

    y_prompt = xp.reshape(bsz, seq, d)
    y_sample = xs[N_META:].reshape(bd, n_tok, d)
    return (y_prompt, y_sample) + tuple(jnp.stack(o, axis=0) for o in outs)
```

```python
import functools
import math

import jax
import jax.numpy as jnp
from jax import lax
from jax.experimental import pallas as pl
from jax.experimental.pallas import tpu as pltpu

F32 = jnp.float32
BF16 = jnp.bfloat16

N_META = 16
N_HEADS = 16
HEAD_DIM = 128
CONV_WIDTH = 31
PAGE_SIZE = 128
FORGET_SPLIT = N_HEADS
LN_EPS = 1e-5
NEG_INF = -1e30
LOG2_E = math.log2(math.e)
QUERY_SCALE = HEAD_DIM ** -0.5 * LOG2_E

V7X_VMEM_LIMIT_BYTES = 56 * 1024 * 1024
V7X_LANES = 128
V7X_SUBLANES = 8
NEW_TOKENS = 8
CUMSUM_ROW_ALIGN = 16
DECODE_PAGES_PER_STEP = 8
DECODE_SEQS_IN_ATTN = 16
DECODE_SEQS_IN_CONV = 16
CONV_ROW_TILE = 64


def _cparams(semantics):
    return pltpu.CompilerParams(dimension_semantics=semantics, vmem_limit_bytes=V7X_VMEM_LIMIT_BYTES)


def _row_tile(rows, preferred):
    best = None
    for t in range(16, min(rows, preferred) + 1, 16):
        if rows % t == 0:
            best = t
    return best if best is not None else rows


def _layer_norm(y, g, b):
    mu = jnp.mean(y, axis=-1, keepdims=True)
    d = y - mu
    var = jnp.mean(d * d, axis=-1, keepdims=True)
    return d * lax.rsqrt(var + LN_EPS) * g + b


def _log_sigmoid(z):
    return jnp.minimum(z, 0.0) - jnp.log1p(jnp.exp(-jnp.abs(z)))


def _split3(x):
    hi = x.astype(BF16)
    r1 = x - hi.astype(F32)
    mid = r1.astype(BF16)
    lo = (r1 - mid.astype(F32)).astype(BF16)
    return hi, mid, lo


def _dot01(sel, x):
    hi, mid, lo = _split3(x)
    out = jnp.dot(sel, lo, preferred_element_type=F32)
    out = out + jnp.dot(sel, mid, preferred_element_type=F32)
    return out + jnp.dot(sel, hi, preferred_element_type=F32)


def _ffn_kernel(x_ref, wa_ref, wb_ref, wo_ref, g_ref, b_ref, *rest, alpha, n_f, emit_bf16):
    if emit_bf16:
        o_ref, ob_ref, xb_ref, acc_ref = rest
    else:
        o_ref, xb_ref, acc_ref = rest
    f = pl.program_id(1)

    @pl.when(f == 0)
    def _init():
        xb_ref[...] = x_ref[...].astype(BF16)
        acc_ref[...] = jnp.zeros_like(acc_ref)

    xb = xb_ref[...]
    a = jnp.dot(xb, wa_ref[...], preferred_element_type=F32)
    b = jnp.dot(xb, wb_ref[...], preferred_element_type=F32)
    h = (a * jax.nn.sigmoid(a) * b).astype(BF16)
    acc_ref[...] += jnp.dot(h, wo_ref[...], preferred_element_type=F32)

    @pl.when(f == n_f - 1)
    def _finish():
        y = alpha * x_ref[...] + 0.5 * acc_ref[...]
        o = _layer_norm(y, g_ref[...], b_ref[...])
        o_ref[...] = o
        if emit_bf16:
            ob_ref[...] = o.astype(BF16)


def _ffn(x, w_in, w_out, g, b, *, alpha, emit_bf16, tm_pref=512, tf=512):
    rows, d = x.shape
    d_ff = w_out.shape[0]
    tm = _row_tile(rows, tm_pref)
    n_f = d_ff // tf
    out_shape = [jax.ShapeDtypeStruct((rows, d), F32)]
    out_specs = [pl.BlockSpec((tm, d), lambda i, f: (i, 0))]
    if emit_bf16:
        out_shape.append(jax.ShapeDtypeStruct((rows, d), BF16))
        out_specs.append(pl.BlockSpec((tm, d), lambda i, f: (i, 0)))
    res = pl.pallas_call(
        functools.partial(_ffn_kernel, alpha=alpha, n_f=n_f, emit_bf16=emit_bf16),
        grid=(rows // tm, n_f),
        in_specs=[
            pl.BlockSpec((tm, d), lambda i, f: (i, 0)),
            pl.BlockSpec((d, tf), lambda i, f: (0, f)),
            pl.BlockSpec((d, tf), lambda i, f: (0, f + n_f)),
            pl.BlockSpec((tf, d), lambda i, f: (f, 0)),
            pl.BlockSpec((1, d), lambda i, f: (0, 0)),
            pl.BlockSpec((1, d), lambda i, f: (0, 0)),
        ],
        out_specs=out_specs,
        out_shape=out_shape,
        scratch_shapes=[pltpu.VMEM((tm, d), BF16), pltpu.VMEM((tm, d), F32)],
        compiler_params=_cparams(("parallel", "arbitrary")),
        name="ffn_sublayer",
    )(x, w_in, w_in, w_out, g, b)
    return res if emit_bf16 else res[0]


def _proj_kernel(x_ref, *refs, mode, n_prefix):
    x = x_ref[0]
    if mode == "glu":
        wa_ref, wg_ref, ba_ref, bg_ref = refs[:4]
        refs = refs[4:]
        za = jnp.dot(x, wa_ref[...], preferred_element_type=F32) + ba_ref[...]
        zg = jnp.dot(x, wg_ref[...], preferred_element_type=F32) + bg_ref[...]
        z = za * jax.nn.sigmoid(zg)
    else:
        w_ref, b_ref = refs[:2]
        refs = refs[2:]
        z = jnp.dot(x, w_ref[...], preferred_element_type=F32) + b_ref[...]
        if mode == "log_sigmoid":
            z = _log_sigmoid(z)
        elif mode == "query":
            z = z * QUERY_SCALE
    if n_prefix:
        pre_ref, o_ref = refs
        o_ref[0, :n_prefix, :] = pre_ref[...].astype(o_ref.dtype)
        o_ref[0, n_prefix:, :] = z.astype(o_ref.dtype)
    else:
        (o_ref,) = refs
        o_ref[0] = z.astype(o_ref.dtype)


def _proj(x, w, bias, *, mode, out_dtype, prefix=None, tn=512):
    bsz, rows, d = x.shape
    n = w.shape[1] // 2 if mode == "glu" else w.shape[1]
    tn = min(tn, n)
    n_t = n // tn
    n_prefix = 0 if prefix is None else prefix.shape[0]
    in_specs = [pl.BlockSpec((1, rows, d), lambda b, j: (b, 0, 0))]
    args = [x]
    if mode == "glu":
        in_specs += [
            pl.BlockSpec((d, tn), lambda b, j: (0, j)),
            pl.BlockSpec((d, tn), lambda b, j: (0, j + n_t)),
            pl.BlockSpec((1, tn), lambda b, j: (0, j)),
            pl.BlockSpec((1, tn), lambda b, j: (0, j + n_t)),
        ]
        args += [w, w, bias, bias]
    else:
        in_specs += [pl.BlockSpec((d, tn), lambda b, j: (0, j)), pl.BlockSpec((1, tn), lambda b, j: (0, j))]
        args += [w, bias]
    if n_prefix:
        in_specs.append(pl.BlockSpec((n_prefix, tn), lambda b, j: (0, j)))
        args.append(prefix)
    return pl.pallas_call(
        functools.partial(_proj_kernel, mode=mode, n_prefix=n_prefix),
        grid=(bsz, n_t),
        in_specs=in_specs,
        out_specs=pl.BlockSpec((1, n_prefix + rows, tn), lambda b, j: (b, 0, j)),
        out_shape=jax.ShapeDtypeStruct((bsz, n_prefix + rows, n), out_dtype),
        compiler_params=_cparams(("parallel", "arbitrary")),
        name="in_proj_" + mode,
    )(*args)


def _cumsum_kernel(*refs, chunk):
    n_seg = len(refs) // 2
    lanes = refs[0].shape[1]
    r = lax.broadcasted_iota(jnp.int32, (chunk, chunk), 0)
    c = lax.broadcasted_iota(jnp.int32, (chunk, chunk), 1)
    tri = jnp.where(c <= r, 1.0, 0.0).astype(BF16)
    carry = jnp.zeros((1, lanes), F32)
    for x_ref, o_ref in zip(refs[:n_seg], refs[n_seg:]):
        length = x_ref.shape[0]
        for start in range(0, length, chunk):
            size = min(chunk, length - start)
            local = _dot01(tri[:size, :size], x_ref[start:start + size, :])
            o_ref[start:start + size, :] = (local + carry) * LOG2_E
            carry = carry + local[size - 1:size, :]


def _cumsum_rows(*segments, chunk=256):
    lanes = segments[0].shape[1]
    tl = V7X_LANES if lanes % V7X_LANES == 0 else lanes
    specs = [pl.BlockSpec((s.shape[0], tl), lambda i: (0, i)) for s in segments]
    return pl.pallas_call(
        functools.partial(_cumsum_kernel, chunk=chunk),
        grid=(lanes // tl,),
        in_specs=specs,
        out_specs=specs,
        out_shape=[jax.ShapeDtypeStruct(s.shape, F32) for s in segments],
        compiler_params=_cparams(("parallel",)),
        name="logf_cumsum",
    )(*segments)


def _attn_prompt_tiles(q_ref, k_ref, v_ref, dkm_ref, dkr_ref, o_ref, *, tq, tiles):
    n_max = N_META + (max(tiles) + 1) * tq
    kb = k_ref[0, :n_max, :].astype(BF16)
    vb = v_ref[0, :n_max, :].astype(BF16)
    k_meta, v_meta = kb[:N_META], vb[:N_META]
    bias_meta = -dkm_ref[0, 0]
    contract_last = (((1,), (1,)), ((), ()))
    causal = lax.broadcasted_iota(jnp.int32, (tq, tq), 1) <= lax.broadcasted_iota(jnp.int32, (tq, tq), 0)
    for qi in tiles:
        lo, hi = N_META + qi * tq, N_META + (qi + 1) * tq
        q = q_ref[0, qi * tq:(qi + 1) * tq, :]
        blocks = [(lax.dot_general(q, k_meta, contract_last, preferred_element_type=F32) + bias_meta, v_meta)]
        s_own = lax.dot_general(q, kb[lo:hi], contract_last, preferred_element_type=F32)
        s_own = s_own - dkr_ref[0, 0, :, qi * tq:(qi + 1) * tq]
        blocks.append((jnp.where(causal, s_own, NEG_INF), vb[lo:hi]))
        if qi:
            s_past = lax.dot_general(q, kb[N_META:lo], contract_last, preferred_element_type=F32)
            blocks.append((s_past - dkr_ref[0, 0, :, :qi * tq], vb[N_META:lo]))
        m = functools.reduce(jnp.maximum, [jnp.max(s, axis=-1, keepdims=True) for s, _ in blocks])
        denom, o = 0.0, 0.0
        for s, v in blocks:
            p = jnp.exp2(s - m)
            denom = denom + jnp.sum(p, axis=-1, keepdims=True)
            o = o + jnp.dot(p.astype(BF16), v, preferred_element_type=F32)
        o_ref[0, qi * tq:(qi + 1) * tq, :] = (o / denom).astype(o_ref.dtype)


def _attn_prompt_kernel(*refs, tq, tile_groups):
    if len(tile_groups) == 1:
        _attn_prompt_tiles(*refs, tq=tq, tiles=tile_groups[0])
        return
    part = pl.program_id(1) % len(tile_groups)
    for i, tiles in enumerate(tile_groups):
        pl.when(part == i)(functools.partial(_attn_prompt_tiles, *refs, tq=tq, tiles=tiles))


def _attn_prompt(q, k, v, dk_meta, dk_real, *, tq_pref=256, **decode):
    bsz, seq, _ = q.shape
    length = k.shape[1]
    tq = _row_tile(seq, tq_pref)
    n_tiles = seq // tq
    if n_tiles % 4 == 0:
        tile_groups = [[], []]
        for i in range(n_tiles // 2):
            tile_groups[i % 2] += [i, n_tiles - 1 - i]
    elif n_tiles % 2 == 0:
        tile_groups = [list(range(n_tiles // 2)), list(range(n_tiles // 2, n_tiles))]
    else:
        tile_groups = [list(range(n_tiles))]
    assert sorted(t for g in tile_groups for t in g) == list(range(n_tiles))
    ns = len(tile_groups)
    (out,), dec_out, n_seqs = _call_with_decode(
        functools.partial(_attn_prompt_kernel, tq=tq, tile_groups=tile_groups),
        grid=(bsz, N_HEADS * ns),
        in_specs=[
            pl.BlockSpec((1, seq, HEAD_DIM), lambda b, g: (b, 0, g // ns)),
            pl.BlockSpec((1, length, HEAD_DIM), lambda b, g: (b, 0, g // ns)),
            pl.BlockSpec((1, length, HEAD_DIM), lambda b, g: (b, 0, g // ns)),
            pl.BlockSpec((1, 1, 1, N_META), lambda b, g: (b, g // ns, 0, 0)),
            pl.BlockSpec((1, 1, 1, seq), lambda b, g: (b, g // ns, 0, 0)),
        ],
        out_specs=[pl.BlockSpec((1, seq, HEAD_DIM), lambda b, g: (b, 0, g // ns))],
        out_shape=[jax.ShapeDtypeStruct(q.shape, BF16)],
        scratch_shapes=[],
        operands=(q, k, v, dk_meta, dk_real),
        name="attn_prompt",
        **decode,
    )
    return out, dec_out, n_seqs


def _conv_taps_by_shift(window, cw_ref, r0, c0, off, n_rows):
    sub = V7X_SUBLANES
    n_out = n_rows // sub
    n_win = -(-(n_rows + off + CONV_WIDTH - 1) // sub)
    win = window[r0:r0 + n_win * sub, c0:c0 + V7X_LANES].reshape(n_win, sub, V7X_LANES)
    sublane = lax.broadcasted_iota(jnp.int32, (1, sub, V7X_LANES), 1)
    out = jnp.zeros((n_out, sub, V7X_LANES), F32)
    for shift in range(sub):
        taps = [w for w in range(CONV_WIDTH) if (off + w) % sub == shift]
        if not taps:
            continue
        rotated = win if shift == 0 else pltpu.roll(win, sub - shift, axis=1)
        n_blocks = n_out if shift == 0 else n_out + 1
        part = jnp.zeros((n_blocks, sub, V7X_LANES), F32)
        for w in taps:
            a = (off + w) // sub
            part = part + cw_ref[w:w + 1, c0:c0 + V7X_LANES][None] * rotated[a:a + n_blocks]
        if shift == 0:
            out = out + part
        else:
            out = out + jnp.where(sublane < sub - shift, part[:n_out], part[1:])
    return out.reshape(n_rows, V7X_LANES)


def _conv_kernel(u_ref, cw_ref, cb_ref, g_ref, b_ref, o_ref, h_ref, *, tr, off, sub_rows):
    channels = o_ref.shape[2]
    base = pl.multiple_of(pl.program_id(1) * tr, V7X_SUBLANES) if tr % V7X_SUBLANES == 0 else 0
    window = u_ref.at[0, pl.ds(base, tr + off + CONV_WIDTH - 1), :]
    for c0 in range(0, channels, V7X_LANES):
        for r0 in range(0, tr, sub_rows):
            if sub_rows % V7X_SUBLANES == 0:
                acc = _conv_taps_by_shift(window, cw_ref, r0, c0, off, sub_rows)
            else:
                acc = jnp.zeros((sub_rows, V7X_LANES), F32)
                for w in range(CONV_WIDTH):
                    rows = window[r0 + off + w:r0 + off + w + sub_rows, c0:c0 + V7X_LANES]
                    acc = acc + cw_ref[w:w + 1, c0:c0 + V7X_LANES] * rows
            h_ref[r0:r0 + sub_rows, c0:c0 + V7X_LANES] = acc
    hn = _layer_norm(h_ref[...] + cb_ref[...], g_ref[...], b_ref[...])
    o_ref[0] = (hn * jax.nn.sigmoid(hn)).astype(o_ref.dtype)


def _conv_branch(hist, conv_w, conv_b, ln_g, ln_b, *, n_out, off, tr_pref=CONV_ROW_TILE, **decode):
    bsz, hist_rows, channels = hist.shape
    tr = _row_tile(n_out, tr_pref) if n_out % 16 == 0 else n_out
    assert hist_rows >= n_out + off + CONV_WIDTH - 1
    sub_rows = math.gcd(tr, 64)
    (out,), dec_out, n_seqs = _call_with_decode(
        functools.partial(_conv_kernel, tr=tr, off=off, sub_rows=sub_rows),
        grid=(bsz, n_out // tr),
        in_specs=[
            pl.BlockSpec((1, hist_rows, channels), lambda b, r: (b, 0, 0)),
            pl.BlockSpec((CONV_WIDTH, channels), lambda b, r: (0, 0)),
            pl.BlockSpec((1, channels), lambda b, r: (0, 0)),
            pl.BlockSpec((1, channels), lambda b, r: (0, 0)),
            pl.BlockSpec((1, channels), lambda b, r: (0, 0)),
        ],
        out_specs=[pl.BlockSpec((1, tr, channels), lambda b, r: (b, r, 0))],
        out_shape=[jax.ShapeDtypeStruct((bsz, n_out, channels), BF16)],
        scratch_shapes=[pltpu.VMEM((tr, channels), F32)],
        operands=(hist, conv_w, conv_b, ln_g, ln_b),
        name="conv_branch",
        **decode,
    )
    return out, dec_out, n_seqs


def _merge_kernel(x_ref, hc_ref, att_ref, wgc_ref, wga_ref, wco_ref, wap_ref, bgc_ref, bga_ref, o_ref):
    x = x_ref[...]
    g_conv = jax.nn.sigmoid(jnp.dot(x, wgc_ref[...], preferred_element_type=F32) + bgc_ref[...])
    g_att = jax.nn.sigmoid(jnp.dot(x, wga_ref[...], preferred_element_type=F32) + bga_ref[...])
    conv_d = jnp.dot(hc_ref[...], wco_ref[...], preferred_element_type=F32)
    att_d = jnp.dot(att_ref[...], wap_ref[...], preferred_element_type=F32)
    o_ref[...] = (g_conv * conv_d + g_att * att_d).astype(o_ref.dtype)


def _merge(xb, h_conv, att, w_gc, w_ga, w_co, w_ap, b_gc, b_ga, *, tm_pref=1024, tn=512):
    rows, d = xb.shape
    d_conv = h_conv.shape[1]
    d_att = att.shape[1]
    tm = _row_tile(rows, tm_pref)
    return pl.pallas_call(
        _merge_kernel,
        grid=(rows // tm, d // tn),
        in_specs=[
            pl.BlockSpec((tm, d), lambda i, j: (i, 0)),
            pl.BlockSpec((tm, d_conv), lambda i, j: (i, 0)),
            pl.BlockSpec((tm, d_att), lambda i, j: (i, 0)),
            pl.BlockSpec((d, tn), lambda i, j: (0, j)),
            pl.BlockSpec((d, tn), lambda i, j: (0, j)),
            pl.BlockSpec((d_conv, tn), lambda i, j: (0, j)),
            pl.BlockSpec((d_att, tn), lambda i, j: (0, j)),
            pl.BlockSpec((1, tn), lambda i, j: (0, j)),
            pl.BlockSpec((1, tn), lambda i, j: (0, j)),
        ],
        out_specs=pl.BlockSpec((tm, tn), lambda i, j: (i, j)),
        out_shape=jax.ShapeDtypeStruct((rows, d), BF16),
        compiler_params=_cparams(("parallel", "arbitrary")),
        name="branch_merge",
    )(xb, h_conv, att, w_gc, w_ga, w_co, w_ap, b_gc, b_ga)


def _mix_out_kernel(x_ref, m_ref, wo_ref, g_ref, b_ref, o_ref, *, alpha):
    y = alpha * x_ref[...] + jnp.dot(m_ref[...], wo_ref[...], preferred_element_type=F32)
    o_ref[...] = _layer_norm(y, g_ref[...], b_ref[...])


def _mix_out(x, m, w_o, g, b, *, alpha, tm_pref=512):
    rows, d = x.shape
    tm = _row_tile(rows, tm_pref)
    return pl.pallas_call(
        functools.partial(_mix_out_kernel, alpha=alpha),
        grid=(rows // tm,),
        in_specs=[
            pl.BlockSpec((tm, d), lambda i: (i, 0)),
            pl.BlockSpec((tm, d), lambda i: (i, 0)),
            pl.BlockSpec((d, d), lambda i: (0, 0)),
            pl.BlockSpec((1, d), lambda i: (0, 0)),
            pl.BlockSpec((1, d), lambda i: (0, 0)),
        ],
        out_specs=pl.BlockSpec((tm, d), lambda i: (i, 0)),
        out_shape=jax.ShapeDtypeStruct((rows, d), F32),
        compiler_params=_cparams(("parallel",)),
        name="mixer_out",
    )(x, m, w_o, g, b)


def _gather_pages_kernel(pt_ref, *refs, pages_per_step, seqs_per_step):
    del pt_ref
    o_ref = refs[-1]
    page = refs[0].shape[1]
    for p in range(pages_per_step):
        pieces = [refs[s * pages_per_step + p][0] for s in range(seqs_per_step)]
        o_ref[p * page:(p + 1) * page, :] = jnp.concatenate(pieces, axis=1)


def _gather_logf_pages(cache_logf, page_table, *, pages_per_step=8):
    bd, n_pages = page_table.shape
    _, page, heads = cache_logf.shape
    pp = math.gcd(n_pages, pages_per_step)
    sps = math.gcd(bd, V7X_LANES // heads)
    if sps * heads != V7X_LANES:
        sps = bd

    def page_map(s, p):
        return lambda g, j, pt: (pt[g * sps + s, j * pp + p], 0, 0)

    return pl.pallas_call(
        functools.partial(_gather_pages_kernel, pages_per_step=pp, seqs_per_step=sps),
        grid_spec=pltpu.PrefetchScalarGridSpec(
            num_scalar_prefetch=1,
            grid=(bd // sps, n_pages // pp),
            in_specs=[pl.BlockSpec((1, page, heads), page_map(s, p)) for s in range(sps) for p in range(pp)],
            out_specs=pl.BlockSpec((pp * page, sps * heads), lambda g, j, pt: (j, g)),
        ),
        out_shape=jax.ShapeDtypeStruct((n_pages * page, bd * heads), F32),
        compiler_params=_cparams(("parallel", "arbitrary")),
        name="gather_logf_pages",
    )(page_table, *([cache_logf] * (sps * pp)))


_CONTRACT_LAST = (((1,), (1,)), ((), ()))


def _same_head(n_rows, n_cols):
    row_head = lax.broadcasted_iota(jnp.int32, (n_rows, n_cols), 0) % N_HEADS
    col_head = lax.broadcasted_iota(jnp.int32, (n_rows, n_cols), 1) % N_HEADS
    return row_head == col_head


def _decode_update(state, scores, values):
    m_ref, l_ref, acc_ref = state
    m_old = m_ref[...]
    m_new = m_old
    for s in scores:
        m_new = jnp.maximum(m_new, jnp.max(s, axis=-1, keepdims=True))
    corr = jnp.exp2(m_old - m_new)
    l_new = l_ref[...] * corr
    acc = acc_ref[...] * corr
    for s, v in zip(scores, values):
        p = jnp.exp2(s - m_new)
        l_new = l_new + jnp.sum(p, axis=-1, keepdims=True)
        acc = acc + jnp.dot(p.astype(BF16), v, preferred_element_type=F32)
    l_ref[...] = l_new
    acc_ref[...] = acc
    m_ref[...] = m_new


def _decode_begin(q_ref, kn_ref, vn_ref, gn_ref, state):
    m_ref, l_ref, acc_ref = state
    q = q_ref[0]
    n_rows = q.shape[0]
    n_cols = NEW_TOKENS * N_HEADS
    s = lax.dot_general(q, kn_ref[0], _CONTRACT_LAST, preferred_element_type=F32) - gn_ref[0]
    key = lax.broadcasted_iota(jnp.int32, (n_rows, n_cols), 1) // N_HEADS
    tok = lax.broadcasted_iota(jnp.int32, (n_rows, n_cols), 0) // N_HEADS
    s = jnp.where(_same_head(n_rows, n_cols) & (key <= tok), s, NEG_INF)
    m_ref[...] = jnp.full_like(m_ref, -jnp.inf)
    l_ref[...] = jnp.zeros_like(l_ref)
    acc_ref[...] = jnp.zeros_like(acc_ref)
    _decode_update(state, [s], [vn_ref[0]])


def _decode_pages(q_ref, g_ref, k_refs, v_refs, state):
    q = q_ref[0]
    page_rows = PAGE_SIZE * N_HEADS
    page_mask = _same_head(q.shape[0], page_rows)
    scores, values = [], []
    for i, (k_ref, v_ref) in enumerate(zip(k_refs, v_refs)):
        k2 = k_ref[0, 0].reshape(page_rows, HEAD_DIM).astype(BF16)
        s = lax.dot_general(q, k2, _CONTRACT_LAST, preferred_element_type=F32) - g_ref[0, 0, i:i + 1, :]
        scores.append(jnp.where(page_mask, s, NEG_INF))
        values.append(v_ref[0, 0].reshape(page_rows, HEAD_DIM).astype(BF16))
    _decode_update(state, scores, values)


def _decode_end(o_ref, state):
    _, l_ref, acc_ref = state
    o_ref[0] = (acc_ref[...] / l_ref[...]).astype(o_ref.dtype)


def _decode_step(dec_in, o_ref, state, *, pages_per_step, n_groups, lin, n_active, host_body=None):
    q_ref, kn_ref, vn_ref, gn_ref, g_ref = dec_in[:5]
    k_refs = dec_in[5:5 + pages_per_step]
    v_refs = dec_in[5 + pages_per_step:]
    group = lin % n_groups
    always = n_active is None
    active = None if always else lin < n_active

    def guard(cond):
        return cond if always else cond & active

    @pl.when(guard(group == 0))
    def _begin():
        _decode_begin(q_ref, kn_ref, vn_ref, gn_ref, state)

    if host_body is not None:
        host_body()
    if always:
        _decode_pages(q_ref, g_ref, k_refs, v_refs, state)
    else:
        pl.when(active)(lambda: _decode_pages(q_ref, g_ref, k_refs, v_refs, state))

    @pl.when(guard(group == n_groups - 1))
    def _end():
        _decode_end(o_ref, state)


class _DecodeStream:
    def __init__(self, q, k_new, v_new, g_new, g_past, cache_k, cache_v, layer, page_table, pages_per_step):
        bd, n_tok, d_att = q.shape
        n_pages = page_table.shape[1]
        assert n_tok <= NEW_TOKENS
        self.bd, self.n_tok, self.d_att = bd, n_tok, d_att
        self.pp = math.gcd(n_pages, pages_per_step)
        self.n_groups = n_pages // self.pp
        self.n_rows = n_tok * N_HEADS
        self.layer = layer
        self.page_table = page_table
        new_rows = NEW_TOKENS * N_HEADS
        page_rows = PAGE_SIZE * N_HEADS
        pad_t = ((0, 0), (0, NEW_TOKENS - n_tok), (0, 0))
        head_rows = lambda a: jnp.pad(a, pad_t).reshape(bd, new_rows, HEAD_DIM).astype(BF16)
        self.operands = [
            q.reshape(bd, self.n_rows, HEAD_DIM), head_rows(k_new), head_rows(v_new),
            jnp.pad(g_new, pad_t).reshape(bd, 1, new_rows),
            g_past.reshape(bd, self.n_groups, self.pp, page_rows),
        ] + [cache_k] * self.pp + [cache_v] * self.pp
        self.new_rows, self.page_rows = new_rows, page_rows

    def in_specs(self, locate):
        pp, layer = self.pp, self.layer

        def per_seq(*idx):
            return (locate(*idx[:-1])[0], 0, 0)

        def bias(*idx):
            seq, group = locate(*idx[:-1])
            return (seq, group, 0, 0)

        def page(i):
            def index(*idx):
                seq, group = locate(*idx[:-1])
                return (layer, idx[-1][seq, group * pp + i], 0, 0, 0)
            return index

        kv = [pl.BlockSpec((1, 1, PAGE_SIZE, N_HEADS, HEAD_DIM), page(i)) for i in range(pp)]
        return [
            pl.BlockSpec((1, self.n_rows, HEAD_DIM), per_seq),
            pl.BlockSpec((1, self.new_rows, HEAD_DIM), per_seq),
            pl.BlockSpec((1, self.new_rows, HEAD_DIM), per_seq),
            pl.BlockSpec((1, 1, self.new_rows), per_seq),
            pl.BlockSpec((1, 1, pp, self.page_rows), bias),
        ] + kv + kv

    def out_spec(self, locate, seq0):
        return pl.BlockSpec((1, self.n_rows, HEAD_DIM), lambda *idx: (locate(*idx[:-1])[0] - seq0, 0, 0))

    def out_shape(self, n_seq):
        return jax.ShapeDtypeStruct((n_seq, self.n_rows, HEAD_DIM), BF16)

    def scratch_shapes(self):
        return [pltpu.VMEM((self.n_rows, 1), F32), pltpu.VMEM((self.n_rows, 1), F32),
                pltpu.VMEM((self.n_rows, HEAD_DIM), F32)]

    def step_params(self):
        return dict(pages_per_step=self.pp, n_groups=self.n_groups)


def _hosted_kernel(pt_ref, *refs, host_kernel, n_host, n_dec_in, step_params, n_active):
    del pt_ref
    n_in, n_out, n_scratch = n_host
    host_in, refs = refs[:n_in], refs[n_in:]
    dec_in, refs = refs[:n_dec_in], refs[n_dec_in:]
    host_out, refs = refs[:n_out], refs[n_out:]
    dec_out, refs = refs[0], refs[1:]
    host_scratch, state = refs[:n_scratch], refs[n_scratch:]
    lin = pl.program_id(0) * pl.num_programs(1) + pl.program_id(1)
    _decode_step(dec_in, dec_out, state, lin=lin, n_active=n_active, **step_params,
                 host_body=lambda: host_kernel(*host_in, *host_out, *host_scratch))


def _call_with_decode(host_kernel, *, grid, in_specs, out_specs, out_shape, scratch_shapes, operands, name,
                      stream=None, seq0=0, max_seqs=0):
    params = _cparams(("arbitrary", "arbitrary"))
    steps = grid[0] * grid[1]
    n_seqs = 0 if stream is None else min(max_seqs, stream.bd - seq0, steps // stream.n_groups)
    if n_seqs <= 0:
        outs = pl.pallas_call(host_kernel, grid=grid, in_specs=in_specs, out_specs=out_specs, out_shape=out_shape,
                              scratch_shapes=scratch_shapes, compiler_params=params, name=name)(*operands)
        return outs, None, 0
    n_groups = stream.n_groups
    n_active = n_seqs * n_groups

    def locate(i, j):
        lin = jnp.minimum(i * grid[1] + j, n_active - 1)
        return seq0 + lin // n_groups, lin % n_groups

    def with_table(spec):
        return pl.BlockSpec(spec.block_shape, lambda *idx, f=spec.index_map: f(*idx[:-1]))

    dec_in_specs = stream.in_specs(locate)
    kernel_fn = functools.partial(
        _hosted_kernel, host_kernel=host_kernel, n_host=(len(in_specs), len(out_specs), len(scratch_shapes)),
        n_dec_in=len(dec_in_specs), step_params=stream.step_params(), n_active=None if n_active == steps else n_active)
    outs = pl.pallas_call(
        kernel_fn,
        grid_spec=pltpu.PrefetchScalarGridSpec(
            num_scalar_prefetch=1,
            grid=grid,
            in_specs=[with_table(s) for s in in_specs] + dec_in_specs,
            out_specs=[with_table(s) for s in out_specs] + [stream.out_spec(locate, seq0)],
            scratch_shapes=list(scratch_shapes) + stream.scratch_shapes(),
        ),
        out_shape=list(out_shape) + [stream.out_shape(n_seqs)],
        compiler_params=params,
        name=name + "_with_decode",
    )(stream.page_table, *operands, *stream.operands)
    return outs[:-1], outs[-1], n_seqs


def _attn_sample_kernel(pt_ref, *refs, n_dec_in, step_params):
    del pt_ref
    dec_in, o_ref, state = refs[:n_dec_in], refs[n_dec_in], refs[n_dec_in + 1:]
    _decode_step(dec_in, o_ref, state, lin=pl.program_id(0) * pl.num_programs(1) + pl.program_id(1),
                 n_active=None, **step_params)


def _attn_sample(stream, seq0):
    n_seqs = stream.bd - seq0
    locate = lambda b, j: (seq0 + b, j)
    dec_in_specs = stream.in_specs(locate)
    return pl.pallas_call(
        functools.partial(_attn_sample_kernel, n_dec_in=len(dec_in_specs), step_params=stream.step_params()),
        grid_spec=pltpu.PrefetchScalarGridSpec(
            num_scalar_prefetch=1,
            grid=(n_seqs, stream.n_groups),
            in_specs=dec_in_specs,
            out_specs=stream.out_spec(locate, seq0),
            scratch_shapes=stream.scratch_shapes(),
        ),
        out_shape=stream.out_shape(n_seqs),
        compiler_params=_cparams(("arbitrary", "arbitrary")),
        name="attn_sample",
    )(stream.page_table, *stream.operands)


def kernel(x_prompt, x_sample, cache_k, cache_v, cache_logf, state_conv, page_table, meta_tokens,
           w_ffn1_in, w_ffn1_out, ln1_g, ln1_b, w_in, b_in, conv_w, conv_b, conv_ln_g, conv_ln_b,
           w_conv_out, w_att_proj, w_o, ln2_g, ln2_b, w_ffn2_in, w_ffn2_out, ln3_g, ln3_b):
    depth = w_ffn1_in.shape[0]
    assert depth == 1, "the meta rows are carried through a single layer only"
    bsz, seq, d = x_prompt.shape
    bd, n_tok, _ = x_sample.shape
    d_att = N_HEADS * HEAD_DIM
    d_conv = conv_w.shape[2]
    alpha = (2.0 * depth) ** 0.25
    o_q = 2 * d_conv
    o_k, o_v, o_f = o_q + d_att, o_q + 2 * d_att, o_q + 3 * d_att
    o_gc = o_f + FORGET_SPLIT
    o_ga = o_gc + d
    row = lambda a: a.reshape(1, -1)

    xp = x_prompt.reshape(bsz * seq, d)
    xs = jnp.concatenate([meta_tokens.astype(F32), x_sample.reshape(bd * n_tok, d)], axis=0)
    outs = [[] for _ in range(8)]
    for l in range(depth):
        w1i, w1o = w_ffn1_in[l].astype(BF16), w_ffn1_out[l].astype(BF16)
        w2i, w2o = w_ffn2_in[l].astype(BF16), w_ffn2_out[l].astype(BF16)
        wl, bl = w_in[l], b_in[l]
        w_u, b_u = wl[:, :o_q].astype(BF16), row(bl[:o_q])
        w_q, b_q = wl[:, o_q:o_k].astype(BF16), row(bl[o_q:o_k])
        w_k, b_k = wl[:, o_k:o_v].astype(BF16), row(bl[o_k:o_v])
        w_v, b_v = wl[:, o_v:o_f].astype(BF16), row(bl[o_v:o_f])
        w_f, b_f = wl[:, o_f:o_gc].astype(BF16), row(bl[o_f:o_gc])
        w_gc, b_gc = wl[:, o_gc:o_ga].astype(BF16), row(bl[o_gc:o_ga])
        w_ga, b_ga = wl[:, o_ga:].astype(BF16), row(bl[o_ga:])
        w_co, w_ap, w_oo = w_conv_out[l].astype(BF16), w_att_proj[l].astype(BF16), w_o[l].astype(BF16)
        g1, b1, g2, b2, g3, b3 = row(ln1_g[l]), row(ln1_b[l]), row(ln2_g[l]), row(ln2_b[l]), row(ln3_g[l]), row(ln3_b[l])
        cb, cg, cbb = row(conv_b[l]), row(conv_ln_g[l]), row(conv_ln_b[l])

        xp, xp_b = _ffn(xp, w1i, w1o, g1, b1, alpha=alpha, emit_bf16=True)
        xs, xs_b = _ffn(xs, w1i, w1o, g1, b1, alpha=alpha, emit_bf16=True)

        xs_b3 = xs_b[None]
        u_s = _proj(xs_b3, w_u, b_u, mode="glu", out_dtype=F32)[0]
        q_s = _proj(xs_b3, w_q, b_q, mode="query", out_dtype=BF16)[0]
        k_s = _proj(xs_b3, w_k, b_k, mode="linear", out_dtype=F32)[0]
        v_s = _proj(xs_b3, w_v, b_v, mode="linear", out_dtype=F32)[0]
        f_s = _proj(xs_b3, w_f, b_f, mode="log_sigmoid", out_dtype=F32)[0]

        xp_b3 = xp_b.reshape(bsz, seq, d)
        u_pre = jnp.concatenate([jnp.zeros((N_META, d_conv), F32), u_s[:N_META]], axis=0)
        u_p = _proj(xp_b3, w_u, b_u, mode="glu", out_dtype=F32, prefix=u_pre)
        q_p = _proj(xp_b3, w_q, b_q, mode="query", out_dtype=BF16)
        k_p = _proj(xp_b3, w_k, b_k, mode="linear", out_dtype=F32, prefix=k_s[:N_META])
        v_p = _proj(xp_b3, w_v, b_v, mode="linear", out_dtype=F32, prefix=v_s[:N_META])
        f_p = _proj(xp_b3, w_f, b_f, mode="log_sigmoid", out_dtype=F32, prefix=f_s[:N_META])

        length = N_META + seq
        f_lanes = f_p.transpose(1, 0, 2).reshape(length, bsz * N_HEADS)
        dk = _cumsum_rows(f_lanes)[0].reshape(length, bsz, N_HEADS).transpose(1, 2, 0)
        dk_meta = dk[:, :, None, :N_META]
        dk_real = dk[:, :, None, N_META:]

        sl = slice(N_META, None)
        n_pages = page_table.shape[1]
        past = n_pages * PAGE_SIZE
        f_new = jnp.pad(f_s[sl].reshape(bd, n_tok, N_HEADS), ((0, 0), (0, CUMSUM_ROW_ALIGN - n_tok), (0, 0)))
        f_new = f_new.transpose(1, 0, 2).reshape(CUMSUM_ROW_ALIGN, bd * N_HEADS)
        g_past, g_new = _cumsum_rows(_gather_logf_pages(cache_logf[l], page_table), f_new)
        g_past = g_past.reshape(past, bd, N_HEADS).transpose(1, 0, 2)
        g_new = g_new.reshape(CUMSUM_ROW_ALIGN, bd, N_HEADS).transpose(1, 0, 2)[:, :n_tok]
        stream = _DecodeStream(
            q_s[sl].reshape(bd, n_tok, d_att), k_s[sl].reshape(bd, n_tok, d_att), v_s[sl].reshape(bd, n_tok, d_att),
            g_new, g_past.reshape(bd, n_pages, PAGE_SIZE, N_HEADS),
            cache_k, cache_v, l, page_table, DECODE_PAGES_PER_STEP)

        att_p, att_s0, n0 = _attn_prompt(q_p, k_p, v_p, dk_meta, dk_real,
                                         stream=stream, seq0=0, max_seqs=DECODE_SEQS_IN_ATTN)
        hc_p, att_s1, n1 = _conv_branch(u_p, conv_w[l], cb, cg, cbb, n_out=seq, off=2,
                                        stream=stream, seq0=n0, max_seqs=DECODE_SEQS_IN_CONV)
        att_parts = [a for a in (att_s0, att_s1) if a is not None]
        if n0 + n1 < bd:
            att_parts.append(_attn_sample(stream, n0 + n1))
        att_s = jnp.concatenate(att_parts, axis=0)
        m_p = _merge(xp_b, hc_p.reshape(bsz * seq, d_conv), att_p.reshape(bsz * seq, d_att),
                     w_gc, w_ga, w_co, w_ap, b_gc, b_ga)
        xp = _mix_out(xp, m_p, w_oo, g2, b2, alpha=alpha)

        u_new = u_s[sl].reshape(bd, n_tok, d_conv)
        hist = jnp.concatenate([state_conv[l].astype(F32), u_new], axis=1)
        hc_s, _, _ = _conv_branch(hist, conv_w[l], cb, cg, cbb, n_out=n_tok, off=0)
        xs_s = xs[sl]
        m_s = _merge(xs_b[sl], hc_s.reshape(bd * n_tok, d_conv), att_s.reshape(bd * n_tok, d_att),
                     w_gc, w_ga, w_co, w_ap, b_gc, b_ga)
        xs_s = _mix_out(xs_s, m_s, w_oo, g2, b2, alpha=alpha)

        xp = _ffn(xp, w2i, w2o, g3, b3, alpha=alpha, emit_bf16=False)
        xs_s = _ffn(xs_s, w2i, w2o, g3, b3, alpha=alpha, emit_bf16=False)
        xs = jnp.concatenate([meta_tokens.astype(F32), xs_s], axis=0)

        hd_shape = (N_HEADS, HEAD_DIM)
        outs[0].append(k_p.reshape(bsz, length, *hd_shape))
        outs[1].append(v_p.reshape(bsz, length, *hd_shape))
        outs[2].append(f_p)
        outs[3].append(u_p[:, -(CONV_WIDTH - 1):])
        outs[4].append(k_s[sl].reshape(bd, n_tok, *hd_shape))
        outs[5].append(v_s[sl].reshape(bd, n_tok, *hd_shape))
        outs[6].append(f_s[sl].reshape(bd, n_tok, N_HEADS))
        outs[7].append(hist[:, -(CONV_WIDTH - 1):])

    y_prompt = xp.reshape(bsz, seq, d)
    y_sample = xs[N_META:].reshape(bd, n_tok, d)
    return (y_prompt, y_sample) + tuple(jnp.stack(o, axis=0) for o in outs)
```
